```python
import math
import numpy as np
import jax, jax.numpy as jnp
from jax import lax

D_MODEL = 4096
BATCH = 4
SEQ = 2048
DEPTH = 2
DEC_BATCH = 128
DEC_SEQ = 1
PAST_LEN = 16384
PAGE_SIZE = 128

N_BRANCH = 4
D_A = D_MODEL // 4
D_B = D_MODEL // 4
D_C = D_MODEL // 4
D_D = D_MODEL // 4
S5_GROUP = 16
S5_GROUPS = D_A // S5_GROUP
S5_STATE = 64
SCONV_W = 3
CONF_W = 31
SSD_HEADDIM = 64
SSD_HEADS = D_D // SSD_HEADDIM
SSD_GROUPS = 4
SSD_STATE = 128
SSD_CONV_W = 4
SSD_CHUNK = 128
D_XBC = D_D + 2 * SSD_GROUPS * SSD_STATE
N_IN = D_A + 3 * D_B + 2 * D_C + D_D + D_XBC + SSD_HEADS + N_BRANCH * D_MODEL
N_EXPERTS = 32
TOP_K = 4
D_FF = D_MODEL // 2
SWIGLU_LIMIT = 7.0
SWIGLU_ALPHA = 1.702
MOE_BLOCK = 128
EPS = 1e-5

kernel_name = 'hybrid_s5_sconv_conformer_ssd_moe_step'


def rmsnorm(x, g):
    xf = x.astype(jnp.float32)
    y = xf * lax.rsqrt(jnp.mean(xf * xf, axis=-1, keepdims=True) + EPS)
    return (y * g.astype(jnp.float32)).astype(x.dtype)


def layernorm(x, g, b):
    xf = x.astype(jnp.float32)
    xc = xf - jnp.mean(xf, axis=-1, keepdims=True)
    var = jnp.mean(xc * xc, axis=-1, keepdims=True)
    return (xc * lax.rsqrt(var + EPS) * g.astype(jnp.float32) + b.astype(jnp.float32)).astype(x.dtype)


def causal_dwconv(v, buf, w):
    width = w.shape[0]
    full = jnp.concatenate([buf.astype(v.dtype), v], axis=1)
    out = lax.conv_general_dilated(full, w[:, None, :].astype(v.dtype), window_strides=(1,), padding='VALID',
                                   dimension_numbers=('NWC', 'WIO', 'NWC'), feature_group_count=v.shape[-1])
    return out, full[:, full.shape[1] - (width - 1):]


def _complex_affine_combine(e1, e2):
    a1r, a1i, b1r, b1i = e1
    a2r, a2i, b2r, b2i = e2
    return (a2r * a1r - a2i * a1i, a2r * a1i + a2i * a1r,
            a2r * b1r - a2i * b1i + b2r, a2r * b1i + a2i * b1r + b2i)


def s5_mixer(u, h0_re, h0_im, lam_re, lam_im, log_step, b_re, b_im, c_re, c_im, d_skip, w_glu):
    f32 = jnp.float32
    n, L, _ = u.shape
    ug = u.astype(f32).reshape(n, L, S5_GROUPS, S5_GROUP)
    step = jnp.exp(log_step.astype(f32))[:, None]
    lr = lam_re.astype(f32)
    li = lam_im.astype(f32)
    mag = jnp.exp(lr * step)
    a_re = mag * jnp.cos(li * step)
    a_im = mag * jnp.sin(li * step)
    den = lr * lr + li * li
    q_re = ((a_re - 1.0) * lr + a_im * li) / den
    q_im = (a_im * lr - (a_re - 1.0) * li) / den
    br = b_re.astype(f32)
    bi = b_im.astype(f32)
    bb_re = q_re[..., None] * br - q_im[..., None] * bi
    bb_im = q_re[..., None] * bi + q_im[..., None] * br
    bu_re = jnp.einsum('blgi,gpi->blgp', ug, bb_re)
    bu_im = jnp.einsum('blgi,gpi->blgp', ug, bb_im)
    h0r = h0_re.astype(f32)
    h0i = h0_im.astype(f32)
    bu_re = bu_re.at[:, 0].add(a_re * h0r - a_im * h0i)
    bu_im = bu_im.at[:, 0].add(a_re * h0i + a_im * h0r)
    ar = jnp.broadcast_to(a_re, bu_re.shape)
    ai = jnp.broadcast_to(a_im, bu_re.shape)
    _, _, h_re, h_im = lax.associative_scan(_complex_affine_combine, (ar, ai, bu_re, bu_im), axis=1)
    y = jnp.einsum('blgp,gip->blgi', h_re, c_re.astype(f32)) - jnp.einsum('blgp,gip->blgi', h_im, c_im.astype(f32))
    y = y.reshape(n, L, D_A) + d_skip.astype(f32) * u.astype(f32)
    g = jax.nn.gelu(y)
    out = g * jax.nn.sigmoid(g @ w_glu.astype(f32))
    return out.astype(u.dtype), h_re[:, -1].astype(h0_re.dtype), h_im[:, -1].astype(h0_im.dtype)


def ssd_scan(x, dt, a, b, c, h0):
    n, L = x.shape[:2]
    q = min(SSD_CHUNK, L)
    nc = -(-L // q)
    pad = nc * q - L
    r = SSD_HEADS // SSD_GROUPS

    def chunk(t):
        t = jnp.pad(t, [(0, 0), (0, pad)] + [(0, 0)] * (t.ndim - 2))
        return t.reshape((n, nc, q) + t.shape[2:])

    xc = chunk(x).reshape(n, nc, q, SSD_GROUPS, r, SSD_HEADDIM)
    dtc = chunk(dt).reshape(n, nc, q, SSD_GROUPS, r)
    bc = chunk(b)
    cc = chunk(c)
    cs = jnp.cumsum(dtc * a.reshape(SSD_GROUPS, r), axis=2)
    cs_t = jnp.moveaxis(cs, 2, -1)
    diff = cs_t[..., :, None] - cs_t[..., None, :]
    causal = jnp.tril(jnp.ones((q, q), dtype=bool))
    decay = jnp.exp(jnp.where(causal, diff, -jnp.inf))
    scores = jnp.einsum('bcign,bcjgn->bcgij', cc, bc)
    xdt = xc * dtc[..., None]
    y_intra = jnp.einsum('bcgrij,bcjgrp->bcigrp', scores[:, :, :, None] * decay, xdt)
    to_end = jnp.exp(cs[:, :, -1:] - cs)
    chunk_states = jnp.einsum('bcjgn,bcjgrp->bcgrpn', bc, xdt * to_end[..., None])
    chunk_decay = jnp.exp(cs[:, :, -1])

    def step(h, inp):
        dec, st = inp
        return dec[..., None, None] * h + st, h

    h_last, h_prev = lax.scan(step, h0.reshape(n, SSD_GROUPS, r, SSD_HEADDIM, SSD_STATE),
                              (jnp.moveaxis(chunk_decay, 1, 0), jnp.moveaxis(chunk_states, 1, 0)))
    h_prev = jnp.moveaxis(h_prev, 0, 1)
    y_inter = jnp.einsum('bcign,bcgrpn->bcigrp', cc, h_prev) * jnp.exp(cs)[..., None]
    y = (y_intra + y_inter).reshape(n, nc * q, SSD_HEADS, SSD_HEADDIM)[:, :L]
    return y, h_last.reshape(n, SSD_HEADS, SSD_HEADDIM, SSD_STATE)


def mamba2_mixer(z, xbc, dt_raw, conv_buf, h0, conv_w, conv_b, dt_bias, a_log, d_skip, norm_g):
    f32 = jnp.float32
    n, L, _ = z.shape
    xbc, new_buf = causal_dwconv(xbc, conv_buf, conv_w)
    xbc = jax.nn.silu(xbc + conv_b)
    xs, bs, cs = jnp.split(xbc, [D_D, D_D + SSD_GROUPS * SSD_STATE], axis=-1)
    xs = xs.astype(f32).reshape(n, L, SSD_HEADS, SSD_HEADDIM)
    bs = bs.astype(f32).reshape(n, L, SSD_GROUPS, SSD_STATE)
    cs = cs.astype(f32).reshape(n, L, SSD_GROUPS, SSD_STATE)
    dt = jax.nn.softplus(dt_raw.astype(f32) + dt_bias.astype(f32))
    a = -jnp.exp(a_log.astype(f32))
    y, h_last = ssd_scan(xs, dt, a, bs, cs, h0.astype(f32))
    y = y + d_skip.astype(f32)[:, None] * xs
    y = y.reshape(n, L, D_D) * jax.nn.silu(z.astype(f32))
    yg = y.reshape(n, L, SSD_GROUPS, D_D // SSD_GROUPS)
    yg = yg * lax.rsqrt(jnp.mean(yg * yg, axis=-1, keepdims=True) + EPS)
    y = yg.reshape(n, L, D_D) * norm_g.astype(f32)
    return y.astype(z.dtype), new_buf, h_last.astype(h0.dtype)


def moe_ffn(h, router_w, router_b, w1, b1, w2, b2):
    shp = h.shape
    t = h.reshape(-1, D_MODEL)
    T = t.shape[0]
    logits = (t @ router_w + router_b).astype(jnp.float32)
    top_val, top_idx = lax.top_k(logits, TOP_K)
    weights = jax.nn.softmax(top_val, axis=-1)
    flat_e = top_idx.reshape(-1)
    order = jnp.argsort(flat_e)
    e_sorted = flat_e[order]
    tok = order // TOP_K
    counts = jnp.zeros((N_EXPERTS,), jnp.int32).at[flat_e].add(1)
    padded = (counts + MOE_BLOCK - 1) // MOE_BLOCK * MOE_BLOCK
    start = jnp.cumsum(counts) - counts
    pstart = jnp.cumsum(padded) - padded
    dest = pstart[e_sorted] + jnp.arange(T * TOP_K, dtype=jnp.int32) - start[e_sorted]
    n_blocks = -(-(T * TOP_K) // MOE_BLOCK) + N_EXPERTS
    buf = jnp.zeros((n_blocks * MOE_BLOCK, D_MODEL), t.dtype).at[dest].set(t[tok])
    block_e = jnp.clip(jnp.searchsorted(pstart + padded, jnp.arange(n_blocks, dtype=jnp.int32) * MOE_BLOCK,
                                        side='right'), 0, N_EXPERTS - 1)

    def expert_block(args):
        rows, e = args
        hid = rows @ w1[e] + b1[e]
        glu, lin = jnp.split(hid, 2, axis=-1)
        glu = jnp.minimum(glu, SWIGLU_LIMIT)
        lin = jnp.clip(lin, -SWIGLU_LIMIT, SWIGLU_LIMIT)
        act = glu * jax.nn.sigmoid(SWIGLU_ALPHA * glu) * (lin + 1.0)
        return act @ w2[e] + b2[e]

    out_buf = lax.map(expert_block, (buf.reshape(n_blocks, MOE_BLOCK, D_MODEL), block_e))
    contrib = out_buf.reshape(-1, D_MODEL)[dest] * weights.reshape(-1)[order][:, None].astype(t.dtype)
    y = jax.ops.segment_sum(contrib, tok, num_segments=T)
    return y.reshape(shp)


def mixing_block(h, st, lw):
    s5_re, s5_im, sconv_buf, conf_buf, ssd_buf, ssd_h = st
    n, L, _ = h.shape
    sizes = (D_A, D_B, D_B, D_B, D_C, D_C, D_D, D_XBC, SSD_HEADS)
    proj = h @ lw['w_in']
    u_a, b_gate, c_gate, v_b, v_c, g_c, z_d, xbc_d, dt_d, gate_logits = jnp.split(
        proj, np.cumsum(sizes).tolist(), axis=-1)
    y_a, s5_re_new, s5_im_new = s5_mixer(u_a, s5_re, s5_im, lw['s5_lambda_re'], lw['s5_lambda_im'],
                                         lw['s5_log_step'], lw['s5_b_re'], lw['s5_b_im'], lw['s5_c_re'],
                                         lw['s5_c_im'], lw['s5_d'], lw['s5_w_glu'])
    branch_a = y_a @ lw['w_out_a']
    conv_b, sconv_new = causal_dwconv(c_gate * v_b, sconv_buf, lw['sconv_w'])
    branch_b = (b_gate * conv_b) @ lw['w_out_b']
    glu = v_c * jax.nn.sigmoid(g_c)
    conv_c, conf_new = causal_dwconv(glu, conf_buf, lw['conf_w'])
    y_c = jax.nn.silu(layernorm(conv_c + lw['conf_b'], lw['conf_ln_g'], lw['conf_ln_b']))
    branch_c = y_c @ lw['w_out_c']
    y_d, ssd_buf_new, ssd_h_new = mamba2_mixer(z_d, xbc_d, dt_d, ssd_buf, ssd_h, lw['ssd_conv_w'], lw['ssd_conv_b'],
                                               lw['ssd_dt_bias'], lw['ssd_a_log'], lw['ssd_d'], lw['ssd_norm'])
    branch_d = y_d @ lw['w_out_d']
    gates = jax.nn.sigmoid(gate_logits.astype(jnp.float32)).astype(h.dtype).reshape(n, L, N_BRANCH, D_MODEL)
    merged = gates[:, :, 0] * branch_a + gates[:, :, 1] * branch_b + gates[:, :, 2] * branch_c + gates[:, :, 3] * branch_d
    out = merged @ lw['w_o']
    return out, (s5_re_new, s5_im_new, sconv_new, conf_new, ssd_buf_new, ssd_h_new)


def trunk_layer(x, c, st, lw):
    mod = jax.nn.silu(c) @ lw['w_ada'] + lw['b_ada']
    shift1, scale1, gate1, shift2, scale2, gate2 = jnp.split(mod[:, None, :], 6, axis=-1)
    h = rmsnorm(x, lw['norm_mix']) * (1.0 + scale1) + shift1
    mix, new_st = mixing_block(h, st, lw)
    x = x + gate1 * mix
    h = rmsnorm(x, lw['norm_ffn']) * (1.0 + scale2) + shift2
    x = x + gate2 * moe_ffn(h, lw['router_w'], lw['router_b'], lw['moe_w1'], lw['moe_b1'], lw['moe_w2'], lw['moe_b2'])
    return x, new_st


def setup_inputs(seed: int = 0) -> dict:
    key = jax.random.key(seed)
    keys = list(jax.random.split(key, 64))
    f32 = jnp.float32

    def nrm(shape, scale):
        return jax.random.normal(keys.pop(), shape, f32) * scale

    def unif(shape, lo, hi):
        return jax.random.uniform(keys.pop(), shape, f32, lo, hi)

    dt0 = jnp.exp(unif((DEPTH, SSD_HEADS), math.log(1e-3), math.log(1e-1)))
    return {
        'x_prompt': nrm((BATCH, SEQ, D_MODEL), 1.0),
        'x_sample': nrm((DEC_BATCH, DEC_SEQ, D_MODEL), 1.0),
        'state_s5_re': nrm((DEPTH, DEC_BATCH, S5_GROUPS, S5_STATE), 0.5),
        'state_s5_im': nrm((DEPTH, DEC_BATCH, S5_GROUPS, S5_STATE), 0.5),
        'cache_sconv': nrm((DEPTH, DEC_BATCH, SCONV_W - 1, D_B), 1.0),
        'cache_conformer': nrm((DEPTH, DEC_BATCH, CONF_W - 1, D_C), 1.0),
        'cache_ssd_conv': nrm((DEPTH, DEC_BATCH, SSD_CONV_W - 1, D_XBC), 1.0),
        'state_ssd': nrm((DEPTH, DEC_BATCH, SSD_HEADS, SSD_HEADDIM, SSD_STATE), 0.3),
        'c_prompt': nrm((BATCH, D_MODEL), 1.0),
        'c_sample': nrm((DEC_BATCH, D_MODEL), 1.0),
        'w_ada': nrm((DEPTH, D_MODEL, 6 * D_MODEL), 0.5 * D_MODEL ** -0.5),
        'b_ada': nrm((DEPTH, 6 * D_MODEL), 0.02),
        'norm_mix': 1.0 + nrm((DEPTH, D_MODEL), 0.05),
        'norm_ffn': 1.0 + nrm((DEPTH, D_MODEL), 0.05),
        'w_in': nrm((DEPTH, D_MODEL, N_IN), D_MODEL ** -0.5),
        's5_lambda_re': -0.5 + nrm((DEPTH, S5_GROUPS, S5_STATE), 0.01),
        's5_lambda_im': math.pi * jnp.arange(S5_STATE, dtype=f32) + nrm((DEPTH, S5_GROUPS, S5_STATE), 0.01),
        's5_log_step': unif((DEPTH, S5_GROUPS), math.log(1e-3), math.log(1e-1)),
        's5_b_re': nrm((DEPTH, S5_GROUPS, S5_STATE, S5_GROUP), (2 * S5_GROUP) ** -0.5),
        's5_b_im': nrm((DEPTH, S5_GROUPS, S5_STATE, S5_GROUP), (2 * S5_GROUP) ** -0.5),
        's5_c_re': nrm((DEPTH, S5_GROUPS, S5_GROUP, S5_STATE), S5_STATE ** -0.5),
        's5_c_im': nrm((DEPTH, S5_GROUPS, S5_GROUP, S5_STATE), S5_STATE ** -0.5),
        's5_d': nrm((DEPTH, D_A), 1.0),
        's5_w_glu': nrm((DEPTH, D_A, D_A), D_A ** -0.5),
        'w_out_a': nrm((DEPTH, D_A, D_MODEL), D_A ** -0.5),
        'sconv_w': nrm((DEPTH, SCONV_W, D_B), SCONV_W ** -0.5),
        'w_out_b': nrm((DEPTH, D_B, D_MODEL), D_B ** -0.5),
        'conf_w': nrm((DEPTH, CONF_W, D_C), CONF_W ** -0.5),
        'conf_b': nrm((DEPTH, D_C), 0.02),
        'conf_ln_g': 1.0 + nrm((DEPTH, D_C), 0.05),
        'conf_ln_b': nrm((DEPTH, D_C), 0.02),
        'w_out_c': nrm((DEPTH, D_C, D_MODEL), D_C ** -0.5),
        'ssd_conv_w': nrm((DEPTH, SSD_CONV_W, D_XBC), SSD_CONV_W ** -0.5),
        'ssd_conv_b': nrm((DEPTH, D_XBC), 0.02),
        'ssd_dt_bias': dt0 + jnp.log(-jnp.expm1(-dt0)),
        'ssd_a_log': jnp.log(unif((DEPTH, SSD_HEADS), 1.0, 16.0)),
        'ssd_d': 1.0 + nrm((DEPTH, SSD_HEADS), 0.1),
        'ssd_norm': 1.0 + nrm((DEPTH, D_D), 0.05),
        'w_out_d': nrm((DEPTH, D_D, D_MODEL), D_D ** -0.5),
        'w_o': nrm((DEPTH, D_MODEL, D_MODEL), D_MODEL ** -0.5),
        'router_w': nrm((DEPTH, D_MODEL, N_EXPERTS), D_MODEL ** -0.5),
        'router_b': nrm((DEPTH, N_EXPERTS), 0.01),
        'moe_w1': nrm((DEPTH, N_EXPERTS, D_MODEL, 2 * D_FF), D_MODEL ** -0.5),
        'moe_b1': nrm((DEPTH, N_EXPERTS, 2 * D_FF), 0.01),
        'moe_w2': nrm((DEPTH, N_EXPERTS, D_FF, D_MODEL), D_FF ** -0.5),
        'moe_b2': nrm((DEPTH, N_EXPERTS, D_MODEL), 0.01),
        'norm_final': 1.0 + nrm((D_MODEL,), 0.05),
    }


def reference(x_prompt, x_sample, state_s5_re, state_s5_im, cache_sconv, cache_conformer, cache_ssd_conv, state_ssd,
              c_prompt, c_sample, w_ada, b_ada, norm_mix, norm_ffn, w_in,
              s5_lambda_re, s5_lambda_im, s5_log_step, s5_b_re, s5_b_im, s5_c_re, s5_c_im, s5_d, s5_w_glu, w_out_a,
              sconv_w, w_out_b, conf_w, conf_b, conf_ln_g, conf_ln_b, w_out_c,
              ssd_conv_w, ssd_conv_b, ssd_dt_bias, ssd_a_log, ssd_d, ssd_norm, w_out_d, w_o,
              router_w, router_b, moe_w1, moe_b1, moe_w2, moe_b2, norm_final):
    layer_weights = {
        'w_ada': w_ada, 'b_ada': b_ada, 'norm_mix': norm_mix, 'norm_ffn': norm_ffn, 'w_in': w_in,
        's5_lambda_re': s5_lambda_re, 's5_lambda_im': s5_lambda_im, 's5_log_step': s5_log_step,
        's5_b_re': s5_b_re, 's5_b_im': s5_b_im, 's5_c_re': s5_c_re, 's5_c_im': s5_c_im, 's5_d': s5_d,
        's5_w_glu': s5_w_glu, 'w_out_a': w_out_a, 'sconv_w': sconv_w, 'w_out_b': w_out_b,
        'conf_w': conf_w, 'conf_b': conf_b, 'conf_ln_g': conf_ln_g, 'conf_ln_b': conf_ln_b, 'w_out_c': w_out_c,
        'ssd_conv_w': ssd_conv_w, 'ssd_conv_b': ssd_conv_b, 'ssd_dt_bias': ssd_dt_bias, 'ssd_a_log': ssd_a_log,
        'ssd_d': ssd_d, 'ssd_norm': ssd_norm, 'w_out_d': w_out_d, 'w_o': w_o,
        'router_w': router_w, 'router_b': router_b, 'moe_w1': moe_w1, 'moe_b1': moe_b1,
        'moe_w2': moe_w2, 'moe_b2': moe_b2,
    }
    xp = x_prompt
    xs = x_sample
    nb = x_prompt.shape[0]
    dtp = x_prompt.dtype
    new_p = [[] for _ in range(6)]
    new_s = [[] for _ in range(6)]
    for l in range(DEPTH):
        lw = {k: v[l] for k, v in layer_weights.items()}
        st_p = (jnp.zeros((nb, S5_GROUPS, S5_STATE), dtp), jnp.zeros((nb, S5_GROUPS, S5_STATE), dtp),
                jnp.zeros((nb, SCONV_W - 1, D_B), dtp), jnp.zeros((nb, CONF_W - 1, D_C), dtp),
                jnp.zeros((nb, SSD_CONV_W - 1, D_XBC), dtp),
                jnp.zeros((nb, SSD_HEADS, SSD_HEADDIM, SSD_STATE), dtp))
        st_s = (state_s5_re[l], state_s5_im[l], cache_sconv[l], cache_conformer[l], cache_ssd_conv[l], state_ssd[l])
        xp, sp = trunk_layer(xp, c_prompt, st_p, lw)
        xs, ss = trunk_layer(xs, c_sample, st_s, lw)
        for i in range(6):
            new_p[i].append(sp[i])
            new_s[i].append(ss[i])
    y_prompt = rmsnorm(xp, norm_final)
    y_sample = rmsnorm(xs, norm_final)
    p = [jnp.stack(a, axis=0) for a in new_p]
    s = [jnp.stack(a, axis=0) for a in new_s]
    return (y_prompt, y_sample, p[0], p[1], p[2], p[3], p[4], p[5], s[0], s[1], s[2], s[3], s[4], s[5])
```

```python
import functools

import jax
import jax.numpy as jnp
import numpy as np
from jax import lax
from jax.experimental import pallas as pl
from jax.experimental.pallas import tpu as pltpu

f32 = jnp.float32
bf16 = jnp.bfloat16
i32 = jnp.int32

D = 4096
NB = 4
L = 2048
NS = 128
TP = NB * L
T = TP + NS
DEPTH = 2
DBR = 1024
NMIX = 9 * DBR
NDT = 16
NGATE = 4 * D
S5G = 64
S5C = 16
S5P = 64
SLAB_G = 8
NSLAB = S5G // SLAB_G
SLAB_C = SLAB_G * S5C
SLAB_P = SLAB_G * S5P
NCHUNK = 8
CONF_W = 31
CONF_TT = 256
SSD_H = 16
SSD_P = 64
SSD_G = 4
SSD_N = 128
SSD_Q = 128
NE = 32
TOPK = 4
DFF = 2048
EPS = 1e-5
LANE = 128
ROW_T = 128
BM = 640
TN = 512
MOE_RT = 256
MOE_UT = 5
MOE_RMAX = MOE_RT * MOE_UT
MOE_NP = T * TOPK + NE * MOE_RT
MOE_NU = 64
MOE_TF = 256
MOE_TO = 256
N_P1 = DFF // MOE_TF
N_P2 = D // MOE_TO
VMEM_LIMIT = 56 * 1024 * 1024


def _cp(sem, vmem=VMEM_LIMIT):
    return pltpu.CompilerParams(dimension_semantics=sem, vmem_limit_bytes=vmem)


def _dot(a, b):
    return jnp.dot(a, b, preferred_element_type=f32)


def _dot_tt(a, b):
    return lax.dot_general(a, b, (((0,), (0,)), ((), ())), preferred_element_type=f32)


def _dot_nt(a, b):
    return lax.dot_general(a, b, (((1,), (1,)), ((), ())), preferred_element_type=f32)


def _sigmoid(x):
    return 1.0 / (1.0 + jnp.exp(-x))


def _silu(x):
    return x * _sigmoid(x)


def _softplus(x):
    return jnp.maximum(x, 0.0) + jnp.log1p(jnp.exp(-jnp.abs(x)))


def _ada_kernel(c_ref, w_ref, b_ref, o_ref):
    a = _silu(c_ref[...]).astype(bf16)
    o_ref[...] = _dot(a, w_ref[...].astype(bf16)) + b_ref[...]


def ada_mod(c_all, w_ada, b_ada, l):
    m = c_all.shape[0]
    n = 6 * D
    return pl.pallas_call(
        _ada_kernel,
        grid=(n // TN,),
        in_specs=[
            pl.BlockSpec((m, D), lambda j: (0, 0)),
            pl.BlockSpec((None, D, TN), lambda j: (l, 0, j)),
            pl.BlockSpec((None, 1, TN), lambda j: (l, 0, j)),
        ],
        out_specs=pl.BlockSpec((m, TN), lambda j: (0, j)),
        out_shape=jax.ShapeDtypeStruct((m, n), f32),
        compiler_params=_cp(("arbitrary",)),
        name="ada_mod",
    )(c_all, w_ada, b_ada.reshape(DEPTH, 1, n))


def _mm_f32w_kernel(a_ref, w_ref, o_ref, wbf_ref):
    @pl.when(pl.program_id(1) == 0)
    def _():
        wbf_ref[...] = w_ref[...].astype(bf16)
    o_ref[...] = _dot(a_ref[...], wbf_ref[...]).astype(o_ref.dtype)


def in_proj(h, w_in, l):
    k = h.shape[1]
    return pl.pallas_call(
        _mm_f32w_kernel,
        grid=(NMIX // TN, T // BM),
        in_specs=[
            pl.BlockSpec((BM, k), lambda j, i: (i, 0)),
            pl.BlockSpec((None, k, TN), lambda j, i: (l, 0, j)),
        ],
        out_specs=pl.BlockSpec((BM, TN), lambda j, i: (i, j)),
        out_shape=jax.ShapeDtypeStruct((T, NMIX), f32),
        scratch_shapes=[pltpu.VMEM((k, TN), bf16)],
        compiler_params=_cp(("arbitrary", "arbitrary")),
        name="in_proj",
    )(h, w_in)


def _gate_kernel(a_ref, w_ref, o_ref):
    o_ref[...] = _sigmoid(_dot(a_ref[...], w_ref[...])).astype(o_ref.dtype)


def gate_proj(h, w_gate):
    k = h.shape[1]
    n = w_gate.shape[1]
    return pl.pallas_call(
        _gate_kernel,
        grid=(n // TN, T // BM),
        in_specs=[
            pl.BlockSpec((BM, k), lambda j, i: (i, 0)),
            pl.BlockSpec((k, TN), lambda j, i: (0, j)),
        ],
        out_specs=pl.BlockSpec((BM, TN), lambda j, i: (i, j)),
        out_shape=jax.ShapeDtypeStruct((T, n), bf16),
        compiler_params=_cp(("arbitrary", "arbitrary")),
        name="gate_proj",
    )(h, w_gate)


def _merge_kernel(ya_ref, yb_ref, yc_ref, yd_ref, wa_ref, wb_ref, wc_ref, wd_ref,
                  ga_ref, gb_ref, gc_ref, gd_ref, o_ref, wbf_ref):
    @pl.when(pl.program_id(1) == 0)
    def _():
        for i, w_ref in enumerate((wa_ref, wb_ref, wc_ref, wd_ref)):
            wbf_ref[i] = w_ref[...].astype(bf16)
    acc = None
    for i, (y_ref, g_ref) in enumerate(((ya_ref, ga_ref), (yb_ref, gb_ref), (yc_ref, gc_ref), (yd_ref, gd_ref))):
        term = g_ref[...].astype(f32) * _dot(y_ref[...], wbf_ref[i])
        acc = term if acc is None else acc + term
    o_ref[...] = acc.astype(o_ref.dtype)


def merge_branches(ys, w_outs, gates, l):
    nt = D // TN
    y_specs = [pl.BlockSpec((BM, DBR), lambda j, i: (i, 0)) for _ in range(4)]
    w_specs = [pl.BlockSpec((None, DBR, TN), lambda j, i: (l, 0, j)) for _ in range(4)]
    g_specs = [pl.BlockSpec((BM, TN), functools.partial(lambda j, i, br: (i, br * nt + j), br=br)) for br in range(4)]
    return pl.pallas_call(
        _merge_kernel,
        grid=(nt, T // BM),
        in_specs=y_specs + w_specs + g_specs,
        out_specs=pl.BlockSpec((BM, TN), lambda j, i: (i, j)),
        out_shape=jax.ShapeDtypeStruct((T, D), bf16),
        scratch_shapes=[pltpu.VMEM((4, DBR, TN), bf16)],
        compiler_params=_cp(("arbitrary", "arbitrary")),
        name="merge_branches",
    )(*ys, *w_outs, gates, gates, gates, gates)


def out_proj(merged, w_o, l):
    return pl.pallas_call(
        _mm_f32w_kernel,
        grid=(D // TN, T // BM),
        in_specs=[
            pl.BlockSpec((BM, D), lambda j, i: (i, 0)),
            pl.BlockSpec((None, D, TN), lambda j, i: (l, 0, j)),
        ],
        out_specs=pl.BlockSpec((BM, TN), lambda j, i: (i, j)),
        out_shape=jax.ShapeDtypeStruct((T, D), f32),
        scratch_shapes=[pltpu.VMEM((D, TN), bf16)],
        compiler_params=_cp(("arbitrary", "arbitrary")),
        name="out_proj",
    )(merged, w_o)


def _mod_group(i):
    return jnp.minimum(i // (L // ROW_T), NB)


def _rms_mod(x, g, scale, shift):
    y = x * lax.rsqrt(jnp.mean(x * x, axis=-1, keepdims=True) + EPS) * g
    return y * (1.0 + scale) + shift


def _normmod1_kernel(x_ref, g_ref, sc_ref, sh_ref, wdt_ref, h_ref, dt_ref):
    hb = _rms_mod(x_ref[...], g_ref[...], sc_ref[...], sh_ref[...]).astype(bf16)
    h_ref[...] = hb
    dt_ref[...] = _dot(hb, wdt_ref[...])


def normmod_mix(x, g, mod, w_dt):
    return pl.pallas_call(
        _normmod1_kernel,
        grid=(T // ROW_T,),
        in_specs=[
            pl.BlockSpec((ROW_T, D), lambda i: (i, 0)),
            pl.BlockSpec((1, D), lambda i: (0, 0)),
            pl.BlockSpec((ROW_T, D), lambda i: (_mod_group(i), 1)),
            pl.BlockSpec((ROW_T, D), lambda i: (_mod_group(i), 0)),
            pl.BlockSpec((D, LANE), lambda i: (0, 0)),
        ],
        out_specs=[
            pl.BlockSpec((ROW_T, D), lambda i: (i, 0)),
            pl.BlockSpec((ROW_T, LANE), lambda i: (i, 0)),
        ],
        out_shape=[jax.ShapeDtypeStruct((T, D), bf16), jax.ShapeDtypeStruct((T, LANE), f32)],
        compiler_params=_cp(("arbitrary",)),
        name="normmod_mix",
    )(x, g, mod, mod, w_dt)


def _normmod2_kernel(x_ref, o_ref, g1_ref, g_ref, sc_ref, sh_ref, wr_ref, br_ref,
                     x1_ref, h_ref, ti_ref, tw_ref):
    x1 = x_ref[...] + g1_ref[...] * o_ref[...]
    x1_ref[...] = x1
    h = _rms_mod(x1, g_ref[...], sc_ref[...], sh_ref[...])
    h_ref[...] = h
    logits = _dot(h.astype(bf16), wr_ref[...]) + br_ref[...]
    lane = lax.broadcasted_iota(i32, logits.shape, 1)
    logits = jnp.where(lane < NE, logits, -jnp.inf)
    ti = jnp.zeros(logits.shape, i32)
    tv = jnp.zeros(logits.shape, f32)
    for k in range(TOPK):
        m = jnp.max(logits, axis=-1, keepdims=True)
        idx = jnp.min(jnp.where(logits == m, lane, LANE), axis=-1, keepdims=True)
        ti = jnp.where(lane == k, idx, ti)
        tv = jnp.where(lane == k, m, tv)
        logits = jnp.where(lane == idx, -jnp.inf, logits)
    ex = jnp.where(lane < TOPK, jnp.exp(tv - tv[:, 0:1]), 0.0)
    ti_ref[...] = ti
    tw_ref[...] = ex / jnp.sum(ex, axis=-1, keepdims=True)


def normmod_ffn(x, o, g, mod, w_router, b_router):
    return pl.pallas_call(
        _normmod2_kernel,
        grid=(T // ROW_T,),
        in_specs=[
            pl.BlockSpec((ROW_T, D), lambda i: (i, 0)),
            pl.BlockSpec((ROW_T, D), lambda i: (i, 0)),
            pl.BlockSpec((ROW_T, D), lambda i: (_mod_group(i), 2)),
            pl.BlockSpec((1, D), lambda i: (0, 0)),
            pl.BlockSpec((ROW_T, D), lambda i: (_mod_group(i), 4)),
            pl.BlockSpec((ROW_T, D), lambda i: (_mod_group(i), 3)),
            pl.BlockSpec((D, LANE), lambda i: (0, 0)),
            pl.BlockSpec((1, LANE), lambda i: (0, 0)),
        ],
        out_specs=[
            pl.BlockSpec((ROW_T, D), lambda i: (i, 0)),
            pl.BlockSpec((ROW_T, D), lambda i: (i, 0)),
            pl.BlockSpec((ROW_T, LANE), lambda i: (i, 0)),
            pl.BlockSpec((ROW_T, LANE), lambda i: (i, 0)),
        ],
        out_shape=[jax.ShapeDtypeStruct((T, D), f32), jax.ShapeDtypeStruct((T, D), f32),
                   jax.ShapeDtypeStruct((T, LANE), i32), jax.ShapeDtypeStruct((T, LANE), f32)],
        compiler_params=_cp(("arbitrary",)),
        name="normmod_ffn",
    )(x, o, mod, g, mod, mod, w_router, b_router)


def _final_norm_kernel(x_ref, g_ref, o_ref):
    x = x_ref[...]
    o_ref[...] = x * lax.rsqrt(jnp.mean(x * x, axis=-1, keepdims=True) + EPS) * g_ref[...]


def final_norm(x, g):
    return pl.pallas_call(
        _final_norm_kernel,
        grid=(T // ROW_T,),
        in_specs=[pl.BlockSpec((ROW_T, D), lambda i: (i, 0)), pl.BlockSpec((1, D), lambda i: (0, 0))],
        out_specs=pl.BlockSpec((ROW_T, D), lambda i: (i, 0)),
        out_shape=jax.ShapeDtypeStruct((T, D), f32),
        compiler_params=_cp(("arbitrary",)),
        name="final_norm",
    )(x, g)


def _s5_param_kernel(lr_ref, li_ref, ls_ref, br_ref, bi_ref, ar_ref, ai_ref, bbr_ref, bbi_ref):
    lr = lr_ref[...]
    li = li_ref[...]
    step = jnp.exp(ls_ref[...])
    mag = jnp.exp(lr * step)
    a_re = mag * jnp.cos(li * step)
    a_im = mag * jnp.sin(li * step)
    den = lr * lr + li * li
    q_re = ((a_re - 1.0) * lr + a_im * li) / den
    q_im = (a_im * lr - (a_re - 1.0) * li) / den
    ar_ref[...] = a_re
    ai_ref[...] = a_im
    for i in range(S5C):
        br = br_ref[i]
        bi = bi_ref[i]
        bbr_ref[i] = q_re * br - q_im * bi
        bbi_ref[i] = q_re * bi + q_im * br


def s5_params(lam_re, lam_im, log_step, b_re, b_im):
    b_re_t = jnp.transpose(b_re, (2, 0, 1))
    b_im_t = jnp.transpose(b_im, (2, 0, 1))
    gp = jax.ShapeDtypeStruct((S5G, S5P), f32)
    cgp = jax.ShapeDtypeStruct((S5C, S5G, S5P), f32)
    return pl.pallas_call(_s5_param_kernel, out_shape=[gp, gp, cgp, cgp], name="s5_params")(
        lam_re, lam_im, log_step.reshape(S5G, 1), b_re_t, b_im_t)


def _s5_slab_weights(bb_re, bb_im, c_re, c_im):
    eye = jnp.eye(SLAB_G, dtype=f32)

    def b_blocks(bb):
        t = bb.reshape(S5C, NSLAB, SLAB_G, S5P)
        return jnp.einsum('isgp,gh->sgihp', t, eye).reshape(NSLAB, SLAB_C, SLAB_P)

    def c_blocks(c):
        t = c.reshape(NSLAB, SLAB_G, S5C, S5P)
        return jnp.einsum('sgip,gh->sgphi', t, eye).reshape(NSLAB, SLAB_P, SLAB_C)

    bsup = jnp.concatenate([b_blocks(bb_re), b_blocks(bb_im)], axis=2).astype(bf16)
    csup = jnp.concatenate([c_blocks(c_re), -c_blocks(c_im)], axis=1).astype(bf16)
    return bsup, csup


def _cmul_add(ar, ai, hr, hi, br, bi):
    return ar * hr - ai * hi + br, ar * hi + ai * hr + bi


def _s5_seq_kernel(u_ref, bsup_ref, csup_ref, ar_ref, ai_ref, d_ref, y_ref, sr_ref, si_ref, scr_ref, *, seq):
    q_len = seq // NCHUNK
    nj = SLAB_P // LANE
    u = u_ref[...]
    bu = _dot(u.astype(bf16), bsup_ref[...])
    for j in range(2 * nj):
        scr_ref[j] = bu[:, LANE * j:LANE * (j + 1)]
    ar = [jnp.broadcast_to(ar_ref[:, LANE * j:LANE * (j + 1)], (NCHUNK, LANE)) for j in range(nj)]
    ai = [jnp.broadcast_to(ai_ref[:, LANE * j:LANE * (j + 1)], (NCHUNK, LANE)) for j in range(nj)]

    def scan(init, store):
        def body(q, hs):
            rows = pl.ds(q, NCHUNK, stride=q_len)
            out_r, out_i = [], []
            for j in range(nj):
                nr, ni = _cmul_add(ar[j], ai[j], hs[j], hs[nj + j], scr_ref[j, rows, :], scr_ref[nj + j, rows, :])
                if store:
                    scr_ref[j, rows, :] = nr
                    scr_ref[nj + j, rows, :] = ni
                out_r.append(nr)
                out_i.append(ni)
            return tuple(out_r + out_i)
        return lax.fori_loop(0, q_len, body, init)

    zeros = tuple(jnp.zeros((NCHUNK, LANE), f32) for _ in range(2 * nj))
    ends = scan(zeros, store=False)
    pr = [a[0:1] for a in ar]
    pi = [a[0:1] for a in ai]
    for _ in range(int(np.log2(q_len))):
        pr, pi = ([pr[j] * pr[j] - pi[j] * pi[j] for j in range(nj)],
                  [2.0 * pr[j] * pi[j] for j in range(nj)])
    starts = []
    for j in range(nj):
        hr = jnp.zeros((1, LANE), f32)
        hi = jnp.zeros((1, LANE), f32)
        rr, ri = [hr], [hi]
        for c in range(NCHUNK - 1):
            hr, hi = _cmul_add(pr[j], pi[j], hr, hi, ends[j][c:c + 1], ends[nj + j][c:c + 1])
            rr.append(hr)
            ri.append(hi)
        starts.append((jnp.concatenate(rr, axis=0), jnp.concatenate(ri, axis=0)))
    init = tuple([s[0] for s in starts] + [s[1] for s in starts])
    fin = scan(init, store=True)
    sr_ref[...] = jnp.concatenate([fin[j][NCHUNK - 1:NCHUNK] for j in range(nj)], axis=1)
    si_ref[...] = jnp.concatenate([fin[nj + j][NCHUNK - 1:NCHUNK] for j in range(nj)], axis=1)
    h_all = jnp.concatenate([scr_ref[j] for j in range(2 * nj)], axis=1).astype(bf16)
    y_ref[...] = _dot(h_all, csup_ref[...]) + d_ref[...] * u


def s5_seq(proj, bsup, csup, a_re, a_im, d_skip, nseq=NB, seq=L):
    a_spec = pl.BlockSpec((None, 1, SLAB_P), lambda b, s: (s, 0, 0))
    st_spec = pl.BlockSpec((None, None, 1, SLAB_P), lambda b, s: (b, s, 0, 0))
    st_shape = jax.ShapeDtypeStruct((nseq, NSLAB, 1, SLAB_P), f32)
    y, sr, si = pl.pallas_call(
        functools.partial(_s5_seq_kernel, seq=seq),
        grid=(nseq, NSLAB),
        in_specs=[
            pl.BlockSpec((seq, SLAB_C), lambda b, s: (b, s)),
            pl.BlockSpec((None, SLAB_C, 2 * SLAB_P), lambda b, s: (s, 0, 0)),
            pl.BlockSpec((None, 2 * SLAB_P, SLAB_C), lambda b, s: (s, 0, 0)),
            a_spec, a_spec,
            pl.BlockSpec((1, SLAB_C), lambda b, s: (0, s)),
        ],
        out_specs=[pl.BlockSpec((seq, SLAB_C), lambda b, s: (b, s)), st_spec, st_spec],
        out_shape=[jax.ShapeDtypeStruct((nseq * seq, DBR), f32), st_shape, st_shape],
        scratch_shapes=[pltpu.VMEM((2 * SLAB_P // LANE, seq, LANE), f32)],
        compiler_params=_cp(("arbitrary", "arbitrary")),
        name="s5_seq",
    )(proj, bsup, csup, a_re.reshape(NSLAB, 1, SLAB_P), a_im.reshape(NSLAB, 1, SLAB_P), d_skip)
    return y, sr.reshape(nseq, S5G, S5P), si.reshape(nseq, S5G, S5P)


def _s5_step_kernel(u_ref, h0r_ref, h0i_ref, bsup_ref, csup_ref, ar_ref, ai_ref, d_ref, y_ref, sr_ref, si_ref):
    u = u_ref[...]
    bu = _dot(u.astype(bf16), bsup_ref[...])
    nr, ni = _cmul_add(ar_ref[...], ai_ref[...], h0r_ref[...], h0i_ref[...], bu[:, :SLAB_P], bu[:, SLAB_P:])
    sr_ref[...] = nr
    si_ref[...] = ni
    h = jnp.concatenate([nr, ni], axis=1).astype(bf16)
    y_ref[...] = _dot(h, csup_ref[...]) + d_ref[...] * u


def s5_step(proj, row_block, h0_re, h0_im, bsup, csup, a_re, a_im, d_skip):
    a_spec = pl.BlockSpec((None, 1, SLAB_P), lambda s: (s, 0, 0))
    st_spec = pl.BlockSpec((NS, SLAB_P), lambda s: (0, s))
    st_shape = jax.ShapeDtypeStruct((NS, S5G * S5P), f32)
    return pl.pallas_call(
        _s5_step_kernel,
        grid=(NSLAB,),
        in_specs=[
            pl.BlockSpec((NS, SLAB_C), lambda s: (row_block, s)),
            st_spec, st_spec,
            pl.BlockSpec((None, SLAB_C, 2 * SLAB_P), lambda s: (s, 0, 0)),
            pl.BlockSpec((None, 2 * SLAB_P, SLAB_C), lambda s: (s, 0, 0)),
            a_spec, a_spec,
            pl.BlockSpec((1, SLAB_C), lambda s: (0, s)),
        ],
        out_specs=[pl.BlockSpec((NS, SLAB_C), lambda s: (0, s)), st_spec, st_spec],
        out_shape=[jax.ShapeDtypeStruct((NS, DBR), f32), st_shape, st_shape],
        compiler_params=_cp(("arbitrary",)),
        name="s5_step",
    )(proj, h0_re, h0_im, bsup, csup, a_re.reshape(NSLAB, 1, SLAB_P), a_im.reshape(NSLAB, 1, SLAB_P), d_skip)


def _s5_glu_kernel(y_ref, w_ref, o_ref):
    g = jax.nn.gelu(y_ref[...])
    o_ref[...] = (g * _sigmoid(_dot(g.astype(bf16), w_ref[...].astype(bf16)))).astype(o_ref.dtype)


def s5_glu(y, w_glu, l):
    m = y.shape[0]
    return pl.pallas_call(
        _s5_glu_kernel,
        grid=(m // BM,),
        in_specs=[pl.BlockSpec((BM, DBR), lambda i: (i, 0)), pl.BlockSpec((None, DBR, DBR), lambda i: (l, 0, 0))],
        out_specs=pl.BlockSpec((BM, DBR), lambda i: (i, 0)),
        out_shape=jax.ShapeDtypeStruct((m, DBR), bf16),
        compiler_params=_cp(("arbitrary",)),
        name="s5_glu",
    )(y, w_glu)


SC_CT = 256
SC_HDR = 8


def _sconv_seq_kernel(bg_ref, cg_ref, v_ref, w_ref, o_ref, cache_ref, scr_ref, *, seq):
    p = cg_ref[...] * v_ref[...]
    scr_ref[pl.ds(0, SC_HDR), :] = jnp.zeros((SC_HDR, SC_CT), f32)
    scr_ref[pl.ds(SC_HDR, seq), :] = p
    conv = (w_ref[0:1, :] * scr_ref[pl.ds(SC_HDR - 2, seq), :]
            + w_ref[1:2, :] * scr_ref[pl.ds(SC_HDR - 1, seq), :]
            + w_ref[2:3, :] * p)
    o_ref[...] = (bg_ref[...] * conv).astype(o_ref.dtype)
    cache_ref[...] = scr_ref[pl.ds(SC_HDR + seq - 2, 2), :]


def sconv_seq(proj, w, nseq=NB, seq=L):
    nct = DBR // SC_CT
    return pl.pallas_call(
        functools.partial(_sconv_seq_kernel, seq=seq),
        grid=(nseq, nct),
        in_specs=[
            pl.BlockSpec((seq, SC_CT), lambda b, c: (b, nct + c)),
            pl.BlockSpec((seq, SC_CT), lambda b, c: (b, 2 * nct + c)),
            pl.BlockSpec((seq, SC_CT), lambda b, c: (b, 3 * nct + c)),
            pl.BlockSpec((3, SC_CT), lambda b, c: (0, c)),
        ],
        out_specs=[pl.BlockSpec((seq, SC_CT), lambda b, c: (b, c)),
                   pl.BlockSpec((None, 2, SC_CT), lambda b, c: (b, 0, c))],
        out_shape=[jax.ShapeDtypeStruct((nseq * seq, DBR), bf16), jax.ShapeDtypeStruct((nseq, 2, DBR), f32)],
        scratch_shapes=[pltpu.VMEM((SC_HDR + seq, SC_CT), f32)],
        compiler_params=_cp(("arbitrary", "arbitrary")),
        name="sconv_seq",
    )(proj, proj, proj, w)


def _sconv_step_kernel(bg_ref, cg_ref, v_ref, c0_ref, c1_ref, w_ref, o_ref, p_ref):
    p = cg_ref[...] * v_ref[...]
    p_ref[...] = p
    conv = w_ref[0:1, :] * c0_ref[...] + w_ref[1:2, :] * c1_ref[...] + w_ref[2:3, :] * p
    o_ref[...] = (bg_ref[...] * conv).astype(o_ref.dtype)


def sconv_step(proj, row_block, cache0, cache1, w):
    blk = lambda col: pl.BlockSpec((NS, DBR), lambda i: (row_block, col))
    full = pl.BlockSpec((NS, DBR), lambda i: (0, 0))
    return pl.pallas_call(
        _sconv_step_kernel,
        grid=(1,),
        in_specs=[blk(1), blk(2), blk(3), full, full, pl.BlockSpec((3, DBR), lambda i: (0, 0))],
        out_specs=[full, full],
        out_shape=[jax.ShapeDtypeStruct((NS, DBR), f32), jax.ShapeDtypeStruct((NS, DBR), f32)],
        compiler_params=_cp(("arbitrary",)),
        name="sconv_step",
    )(proj, proj, proj, cache0, cache1, w)


CF_HDR = 32


def _layernorm_silu(x, g, b):
    xc = x - jnp.mean(x, axis=-1, keepdims=True)
    var = jnp.mean(xc * xc, axis=-1, keepdims=True)
    return _silu(xc * lax.rsqrt(var + EPS) * g + b)


def _conf_seq_kernel(v_ref, g_ref, w_ref, cb_ref, lg_ref, lb_ref, o_ref, cache_ref, win_ref, *, n_tiles):
    t = pl.program_id(1)

    @pl.when(t == 0)
    def _():
        win_ref[pl.ds(0, CF_HDR), :] = jnp.zeros((CF_HDR, DBR), f32)

    glu = v_ref[...] * _sigmoid(g_ref[...])
    win_ref[pl.ds(CF_HDR, CONF_TT), :] = glu
    acc = w_ref[CONF_W - 1:CONF_W, :] * glu
    for k in range(CONF_W - 1):
        acc = acc + w_ref[k:k + 1, :] * win_ref[pl.ds(CF_HDR - (CONF_W - 1) + k, CONF_TT), :]
    o_ref[...] = _layernorm_silu(acc + cb_ref[...], lg_ref[...], lb_ref[...]).astype(o_ref.dtype)

    @pl.when(t == n_tiles - 1)
    def _():
        cache_ref[...] = win_ref[pl.ds(CF_HDR + CONF_TT - (CONF_W - 1), CONF_W - 1), :]

    win_ref[pl.ds(0, CF_HDR), :] = win_ref[pl.ds(CONF_TT, CF_HDR), :]


def conf_seq(proj, w, cb, lg, lb, nseq=NB, seq=L):
    n_tiles = seq // CONF_TT
    vec = pl.BlockSpec((1, DBR), lambda b, t: (0, 0))
    return pl.pallas_call(
        functools.partial(_conf_seq_kernel, n_tiles=n_tiles),
        grid=(nseq, n_tiles),
        in_specs=[
            pl.BlockSpec((CONF_TT, DBR), lambda b, t: (b * n_tiles + t, 4)),
            pl.BlockSpec((CONF_TT, DBR), lambda b, t: (b * n_tiles + t, 5)),
            pl.BlockSpec((CONF_W, DBR), lambda b, t: (0, 0)),
            vec, vec, vec,
        ],
        out_specs=[pl.BlockSpec((CONF_TT, DBR), lambda b, t: (b * n_tiles + t, 0)),
                   pl.BlockSpec((None, CONF_W - 1, DBR), lambda b, t: (b, 0, 0))],
        out_shape=[jax.ShapeDtypeStruct((nseq * seq, DBR), bf16),
                   jax.ShapeDtypeStruct((nseq, CONF_W - 1, DBR), f32)],
        scratch_shapes=[pltpu.VMEM((CF_HDR + CONF_TT, DBR), f32)],
        compiler_params=_cp(("arbitrary", "arbitrary")),
        name="conf_seq",
    )(proj, proj, w, cb, lg, lb)


CF_BT = 16


def _conf_step_kernel(v_ref, g_ref, c_ref, w_ref, cb_ref, lg_ref, lb_ref, o_ref, glu_ref):
    glu = v_ref[...] * _sigmoid(g_ref[...])
    glu_ref[...] = glu
    conv = jnp.sum(c_ref[...] * w_ref[0:CONF_W - 1, :][None], axis=1) + w_ref[CONF_W - 1:CONF_W, :] * glu
    o_ref[...] = _layernorm_silu(conv + cb_ref[...], lg_ref[...], lb_ref[...]).astype(o_ref.dtype)


def conf_step(proj, row0, cache, w, cb, lg, lb):
    rb = row0 // CF_BT
    vec = pl.BlockSpec((1, DBR), lambda i: (0, 0))
    row = pl.BlockSpec((CF_BT, DBR), lambda i: (i, 0))
    return pl.pallas_call(
        _conf_step_kernel,
        grid=(NS // CF_BT,),
        in_specs=[
            pl.BlockSpec((CF_BT, DBR), lambda i: (rb + i, 4)),
            pl.BlockSpec((CF_BT, DBR), lambda i: (rb + i, 5)),
            pl.BlockSpec((CF_BT, CONF_W - 1, DBR), lambda i: (i, 0, 0)),
            pl.BlockSpec((CONF_W, DBR), lambda i: (0, 0)),
            vec, vec, vec,
        ],
        out_specs=[row, row],
        out_shape=[jax.ShapeDtypeStruct((NS, DBR), f32), jax.ShapeDtypeStruct((NS, DBR), f32)],
        compiler_params=_cp(("arbitrary",)),
        name="conf_step",
    )(proj, proj, cache, w, cb, lg, lb)


SSD_HDR = 8
SSD_CW = 4
SSD_GC = DBR // SSD_G


def _split3(x):
    hi = x.astype(bf16)
    r1 = x - hi.astype(f32)
    mid = r1.astype(bf16)
    lo = (r1 - mid.astype(f32)).astype(bf16)
    return hi, mid, lo


def _gated_group_norm(y, z, g):
    yz = y * _silu(z)
    outs = []
    for gi in range(SSD_G):
        seg = yz[:, SSD_GC * gi:SSD_GC * (gi + 1)]
        outs.append(seg * lax.rsqrt(jnp.mean(seg * seg, axis=-1, keepdims=True) + EPS))
    return jnp.concatenate(outs, axis=1) * g


def _ssd_seq_kernel(z_ref, x_ref, bc_ref, dt_ref, wx_ref, wbc_ref, bx_ref, bbc_ref, dtb_ref, alog_ref,
                    dch_ref, ng_ref, y_ref, cache_ref, hout_ref, xwin_ref, bcwin_ref, h_ref, *, n_chunks):
    c = pl.program_id(1)
    q = SSD_Q

    @pl.when(c == 0)
    def _():
        xwin_ref[pl.ds(0, SSD_HDR), :] = jnp.zeros((SSD_HDR, DBR), f32)
        bcwin_ref[pl.ds(0, SSD_HDR), :] = jnp.zeros((SSD_HDR, DBR), f32)
        h_ref[...] = jnp.zeros(h_ref.shape, f32)

    def conv(raw_ref, win_ref, w_ref, b_ref):
        raw = raw_ref[...]
        win_ref[pl.ds(SSD_HDR, q), :] = raw
        acc = w_ref[SSD_CW - 1:SSD_CW, :] * raw + b_ref[...]
        for k in range(SSD_CW - 1):
            acc = acc + w_ref[k:k + 1, :] * win_ref[pl.ds(SSD_HDR - (SSD_CW - 1) + k, q), :]
        return _silu(acc)

    xs = conv(x_ref, xwin_ref, wx_ref, bx_ref)
    bc = conv(bc_ref, bcwin_ref, wbc_ref, bbc_ref)

    @pl.when(c == n_chunks - 1)
    def _():
        cache_ref[:, 0:DBR] = xwin_ref[pl.ds(SSD_HDR + q - (SSD_CW - 1), SSD_CW - 1), :]
        cache_ref[:, DBR:2 * DBR] = bcwin_ref[pl.ds(SSD_HDR + q - (SSD_CW - 1), SSD_CW - 1), :]

    xwin_ref[pl.ds(0, SSD_HDR), :] = xwin_ref[pl.ds(q, SSD_HDR), :]
    bcwin_ref[pl.ds(0, SSD_HDR), :] = bcwin_ref[pl.ds(q, SSD_HDR), :]

    lane = lax.broadcasted_iota(i32, (q, LANE), 1)
    dt = jnp.where(lane < SSD_H, _softplus(dt_ref[...] + dtb_ref[...]), 0.0)
    a = -jnp.exp(alog_ref[...])
    dta = jnp.where(lane < SSD_H, dt * a, 0.0)
    ri = lax.broadcasted_iota(i32, (q, q), 0)
    ci = lax.broadcasted_iota(i32, (q, q), 1)
    causal = ri >= ci
    tri = causal.astype(bf16)
    parts = _split3(dta)
    cs = sum(_dot(tri, p) for p in parts)
    cst = sum(_dot_tt(p, (ri <= ci).astype(bf16)) for p in parts)

    y_parts = []
    for g in range(SSD_G):
        bg = bc[:, SSD_N * g:SSD_N * (g + 1)].astype(bf16)
        cg = bc[:, SSD_G * SSD_N + SSD_N * g:SSD_G * SSD_N + SSD_N * (g + 1)].astype(bf16)
        scores = _dot_nt(cg, bg)
        for r in range(SSD_H // SSD_G):
            h = g * (SSD_H // SSD_G) + r
            col = cs[:, h:h + 1]
            row = cst[h:h + 1, :]
            decay = jnp.exp(jnp.where(causal, col - row, -jnp.inf))
            xh = xs[:, SSD_P * h:SSD_P * (h + 1)]
            xdt = xh * dt[:, h:h + 1]
            y_intra = _dot((scores * decay).astype(bf16), xdt.astype(bf16))
            hprev = h_ref[h]
            y_inter = _dot_nt(cg, hprev.astype(bf16)) * jnp.exp(col)
            last = cs[q - 1:q, h:h + 1]
            to_end = jnp.exp(last - col)
            h_ref[h] = jnp.exp(last) * hprev + _dot_tt((xdt * to_end).astype(bf16), bg)
            y_parts.append(y_intra + y_inter)
    y = jnp.concatenate(y_parts, axis=1) + dch_ref[...] * xs
    y_ref[...] = _gated_group_norm(y, z_ref[...], ng_ref[...]).astype(y_ref.dtype)

    @pl.when(c == n_chunks - 1)
    def _():
        hout_ref[...] = h_ref[...]


def ssd_seq(proj, dt_raw, conv_w, conv_b, dt_bias, a_log, d_ch, norm_g, nseq=NB, seq=L):
    n_chunks = seq // SSD_Q
    rowblk = lambda col: pl.BlockSpec((SSD_Q, DBR), lambda b, c: (b * n_chunks + c, col))
    vec = lambda col: pl.BlockSpec((1, DBR), lambda b, c: (0, col))
    vecl = pl.BlockSpec((1, LANE), lambda b, c: (0, 0))
    return pl.pallas_call(
        functools.partial(_ssd_seq_kernel, n_chunks=n_chunks),
        grid=(nseq, n_chunks),
        in_specs=[
            rowblk(6), rowblk(7), rowblk(8),
            pl.BlockSpec((SSD_Q, LANE), lambda b, c: (b * n_chunks + c, 0)),
            pl.BlockSpec((SSD_CW, DBR), lambda b, c: (0, 0)),
            pl.BlockSpec((SSD_CW, DBR), lambda b, c: (0, 1)),
            vec(0), vec(1), vecl, vecl, vec(0), vec(0),
        ],
        out_specs=[
            pl.BlockSpec((SSD_Q, DBR), lambda b, c: (b * n_chunks + c, 0)),
            pl.BlockSpec((None, SSD_CW - 1, 2 * DBR), lambda b, c: (b, 0, 0)),
            pl.BlockSpec((None, SSD_H, SSD_P, SSD_N), lambda b, c: (b, 0, 0, 0)),
        ],
        out_shape=[jax.ShapeDtypeStruct((nseq * seq, DBR), bf16),
                   jax.ShapeDtypeStruct((nseq, SSD_CW - 1, 2 * DBR), f32),
                   jax.ShapeDtypeStruct((nseq, SSD_H, SSD_P, SSD_N), f32)],
        scratch_shapes=[pltpu.VMEM((SSD_HDR + SSD_Q, DBR), f32), pltpu.VMEM((SSD_HDR + SSD_Q, DBR), f32),
                        pltpu.VMEM((SSD_H, SSD_P, SSD_N), f32)],
        compiler_params=_cp(("arbitrary", "arbitrary")),
        name="ssd_seq",
    )(proj, proj, proj, dt_raw, conv_w, conv_w, conv_b, conv_b, dt_bias, a_log, d_ch, norm_g)


SSD_BT = 8


def _ssd_step_kernel(z_ref, x_ref, bc_ref, dt_ref, cx_ref, cbc_ref, h0_ref, wx_ref, wbc_ref, bx_ref, bbc_ref,
                     dtb_ref, alog_ref, dch_ref, ng_ref, y_ref, hout_ref):
    def conv(raw_ref, c_ref, w_ref, b_ref):
        acc = w_ref[SSD_CW - 1:SSD_CW, :] * raw_ref[...] + b_ref[...]
        for k in range(SSD_CW - 1):
            acc = acc + w_ref[k:k + 1, :] * c_ref[k]
        return _silu(acc)

    xs = conv(x_ref, cx_ref, wx_ref, bx_ref)
    bc = conv(bc_ref, cbc_ref, wbc_ref, bbc_ref)
    dt = _softplus(dt_ref[...] + dtb_ref[...])
    da = jnp.exp(dt * (-jnp.exp(alog_ref[...])))
    rows_x = lax.broadcasted_iota(i32, (SSD_BT, SSD_GC), 0)
    rows_n = lax.broadcasted_iota(i32, (SSD_BT, SSD_N), 0)
    hpg = SSD_H // SSD_G
    y_groups = []
    for g in range(SSD_G):
        bg = bc[:, SSD_N * g:SSD_N * (g + 1)].astype(bf16)
        cg = bc[:, SSD_G * SSD_N + SSD_N * g:SSD_G * SSD_N + SSD_N * (g + 1)]
        xdt = jnp.concatenate(
            [xs[:, SSD_P * (g * hpg + r):SSD_P * (g * hpg + r + 1)] * dt[:, g * hpg + r:g * hpg + r + 1]
             for r in range(hpg)], axis=1)
        yg = jnp.zeros((SSD_BT, SSD_GC), f32)
        for b in range(SSD_BT):
            upd = _dot_tt(jnp.where(rows_x == b, xdt, 0.0).astype(bf16), bg)
            hn = []
            for r in range(hpg):
                h = g * hpg + r
                hnew = da[b:b + 1, h:h + 1] * h0_ref[b, h] + upd[SSD_P * r:SSD_P * (r + 1), :]
                hout_ref[b, h] = hnew
                hn.append(hnew)
            hn = jnp.concatenate(hn, axis=0).astype(bf16)
            yg = yg + _dot_nt(jnp.where(rows_n == b, cg, 0.0).astype(bf16), hn)
        y_groups.append(yg)
    y = jnp.concatenate(y_groups, axis=1) + dch_ref[...] * xs
    y_ref[...] = _gated_group_norm(y, z_ref[...], ng_ref[...]).astype(y_ref.dtype)


def ssd_step(proj, dt_raw, row0, cache_t, h0, conv_w, conv_b, dt_bias, a_log, d_ch, norm_g):
    rb = row0 // SSD_BT
    rowblk = lambda col: pl.BlockSpec((SSD_BT, DBR), lambda i: (rb + i, col))
    vec = lambda col: pl.BlockSpec((1, DBR), lambda i: (0, col))
    vecl = pl.BlockSpec((1, LANE), lambda i: (0, 0))
    st = pl.BlockSpec((SSD_BT, SSD_H, SSD_P, SSD_N), lambda i: (i, 0, 0, 0))
    return pl.pallas_call(
        _ssd_step_kernel,
        grid=(NS // SSD_BT,),
        in_specs=[
            rowblk(6), rowblk(7), rowblk(8),
            pl.BlockSpec((SSD_BT, LANE), lambda i: (rb + i, 0)),
            pl.BlockSpec((SSD_CW - 1, SSD_BT, DBR), lambda i: (0, i, 0)),
            pl.BlockSpec((SSD_CW - 1, SSD_BT, DBR), lambda i: (0, i, 1)),
            st,
            pl.BlockSpec((SSD_CW, DBR), lambda i: (0, 0)),
            pl.BlockSpec((SSD_CW, DBR), lambda i: (0, 1)),
            vec(0), vec(1), vecl, vecl, vec(0), vec(0),
        ],
        out_specs=[pl.BlockSpec((SSD_BT, DBR), lambda i: (i, 0)), st],
        out_shape=[jax.ShapeDtypeStruct((NS, DBR), f32),
                   jax.ShapeDtypeStruct((NS, SSD_H, SSD_P, SSD_N), f32)],
        compiler_params=_cp(("arbitrary",)),
        name="ssd_step",
    )(proj, proj, proj, dt_raw, cache_t, cache_t, h0, conv_w, conv_w, conv_b, conv_b,
      dt_bias, a_log, d_ch, norm_g)


def moe_plan(top_idx):
    flat_e = top_idx.reshape(-1)
    onehot = (flat_e[:, None] == jnp.arange(NE, dtype=i32)[None, :]).astype(i32)
    csum = jnp.cumsum(onehot, axis=0)
    rank = jnp.sum(onehot * csum, axis=1) - 1
    counts = csum[-1]
    tiles = (counts + MOE_RT - 1) // MOE_RT
    tile0 = jnp.cumsum(tiles) - tiles
    units = (tiles + MOE_UT - 1) // MOE_UT
    unit_end = jnp.cumsum(units)
    unit0 = unit_end - units
    xpos = tile0[flat_e] * MOE_RT + rank
    ypos = (unit0[flat_e] + rank // MOE_RMAX) * MOE_RMAX + rank % MOE_RMAX
    tok_sorted = jnp.zeros((MOE_NP,), i32).at[xpos].set(jnp.arange(T * TOPK, dtype=i32) // TOPK)
    u = jnp.arange(MOE_NU, dtype=i32)
    n_units = unit_end[-1]
    ue = jnp.minimum(jnp.searchsorted(unit_end, u, side='right').astype(i32), NE - 1)
    last_e = ue[jnp.maximum(n_units - 1, 0)]
    valid = u < n_units
    local = u - unit0[ue]
    unit_e = jnp.where(valid, ue, last_e)
    unit_row0 = jnp.where(valid, (tile0[ue] + local * MOE_UT) * MOE_RT, 0)
    unit_nt = jnp.where(valid, jnp.minimum(MOE_UT, tiles[ue] - local * MOE_UT), 0)
    return tok_sorted, ypos.reshape(T, TOPK), unit_e, unit_row0, unit_nt


def _gather_kernel(idx_ref, h_ref, o_ref, buf_ref, sem):
    def issue(i, carry):
        tok = idx_ref[0, 0, i]
        pltpu.make_async_copy(h_ref.at[pl.ds(tok, 1), :], buf_ref.at[pl.ds(i, 1), :], sem).start()
        return carry
    lax.fori_loop(0, MOE_RT, issue, 0)
    pltpu.make_async_copy(h_ref.at[pl.ds(0, MOE_RT), :], buf_ref, sem).wait()
    o_ref[...] = buf_ref[...].astype(o_ref.dtype)


def moe_gather(h, tok_sorted):
    nt = MOE_NP // MOE_RT
    return pl.pallas_call(
        _gather_kernel,
        grid=(nt,),
        in_specs=[pl.BlockSpec((1, 1, MOE_RT), lambda i: (i, 0, 0), memory_space=pltpu.SMEM),
                  pl.BlockSpec(memory_space=pl.ANY)],
        out_specs=pl.BlockSpec((MOE_RT, D), lambda i: (i, 0)),
        out_shape=jax.ShapeDtypeStruct((MOE_NP, D), bf16),
        scratch_shapes=[pltpu.VMEM((MOE_RT, D), f32), pltpu.SemaphoreType.DMA(())],
        compiler_params=_cp(("arbitrary",)),
        name="moe_gather",
    )(tok_sorted.reshape(nt, 1, MOE_RT), h)


def _ffn_kernel(ue_ref, ur_ref, un_ref, xs_ref, wg_ref, wl_ref, bg_ref, bl_ref, w2_ref, b2_ref, o_ref,
                x_ref, act_ref, wgb_ref, wlb_ref, w2b_ref, sem):
    u = pl.program_id(0)
    s = pl.program_id(1)
    nt = un_ref[u]
    row0 = ur_ref[u]

    def x_copy(r):
        src = xs_ref.at[pl.ds(pl.multiple_of(row0 + r * MOE_RT, MOE_RT), MOE_RT), :]
        return pltpu.make_async_copy(src, x_ref.at[pl.ds(r * MOE_RT, MOE_RT), :], sem.at[r])

    @pl.when(s == 0)
    def _():
        for r in range(MOE_UT):
            @pl.when(r < nt)
            def _():
                x_copy(r).start()
        for r in range(MOE_UT):
            @pl.when(r < nt)
            def _():
                x_copy(r).wait()

    @pl.when((s < N_P1) & (nt > 0))
    def _():
        wgb_ref[...] = wg_ref[...].astype(bf16)
        wlb_ref[...] = wl_ref[...].astype(bf16)

        def tile(r, carry):
            rows = pl.ds(pl.multiple_of(r * MOE_RT, MOE_RT), MOE_RT)
            x = x_ref[rows, :]
            glu = jnp.minimum(_dot(x, wgb_ref[...]) + bg_ref[...], 7.0)
            lin = jnp.clip(_dot(x, wlb_ref[...]) + bl_ref[...], -7.0, 7.0)
            act = (glu * _sigmoid(1.702 * glu) * (lin + 1.0)).astype(bf16)
            for j in range(N_P1):
                @pl.when(s == j)
                def _():
                    act_ref[rows, MOE_TF * j:MOE_TF * (j + 1)] = act
            return carry
        lax.fori_loop(0, nt, tile, 0)

    @pl.when(s >= N_P1)
    def _():
        @pl.when(nt > 0)
        def _():
            w2b_ref[...] = w2_ref[...].astype(bf16)

        def tile(r, carry):
            rows = pl.ds(pl.multiple_of(r * MOE_RT, MOE_RT), MOE_RT)
            o_ref[rows, :] = _dot(act_ref[rows, :], w2b_ref[...]) + b2_ref[...]
            return carry
        lax.fori_loop(0, nt, tile, 0)

        def fill(r, carry):
            rows = pl.ds(pl.multiple_of(r * MOE_RT, MOE_RT), MOE_RT)
            o_ref[rows, :] = jnp.zeros((MOE_RT, MOE_TO), f32)
            return carry
        lax.fori_loop(nt, MOE_UT, fill, 0)


def moe_ffn(xs, unit_e, unit_row0, unit_nt, w1, b1, w2, b2, l):
    def c1(u, s, un):
        return jnp.where(un[u] > 0, jnp.minimum(s, N_P1 - 1), N_P1 - 1)

    def c2(u, s, un):
        return jnp.where(un[u] > 0, jnp.maximum(s - N_P1, 0), N_P2 - 1)

    nglu = DFF // MOE_TF
    grid_spec = pltpu.PrefetchScalarGridSpec(
        num_scalar_prefetch=3,
        grid=(MOE_NU, N_P1 + N_P2),
        in_specs=[
            pl.BlockSpec(memory_space=pl.ANY),
            pl.BlockSpec((None, None, D, MOE_TF), lambda u, s, ue, ur, un: (l, ue[u], 0, c1(u, s, un))),
            pl.BlockSpec((None, None, D, MOE_TF), lambda u, s, ue, ur, un: (l, ue[u], 0, nglu + c1(u, s, un))),
            pl.BlockSpec((None, None, 1, MOE_TF), lambda u, s, ue, ur, un: (l, ue[u], 0, c1(u, s, un))),
            pl.BlockSpec((None, None, 1, MOE_TF), lambda u, s, ue, ur, un: (l, ue[u], 0, nglu + c1(u, s, un))),
            pl.BlockSpec((None, None, DFF, MOE_TO), lambda u, s, ue, ur, un: (l, ue[u], 0, c2(u, s, un))),
            pl.BlockSpec((None, None, 1, MOE_TO), lambda u, s, ue, ur, un: (l, ue[u], 0, c2(u, s, un))),
        ],
        out_specs=pl.BlockSpec((MOE_RMAX, MOE_TO), lambda u, s, ue, ur, un: (u, jnp.maximum(s - N_P1, 0))),
        scratch_shapes=[
            pltpu.VMEM((MOE_RMAX, D), bf16),
            pltpu.VMEM((MOE_RMAX, DFF), bf16),
            pltpu.VMEM((D, MOE_TF), bf16),
            pltpu.VMEM((D, MOE_TF), bf16),
            pltpu.VMEM((DFF, MOE_TO), bf16),
            pltpu.SemaphoreType.DMA((MOE_UT,)),
        ],
    )
    b1r = b1.reshape(DEPTH, NE, 1, 2 * DFF)
    b2r = b2.reshape(DEPTH, NE, 1, D)
    return pl.pallas_call(
        _ffn_kernel,
        grid_spec=grid_spec,
        out_shape=jax.ShapeDtypeStruct((MOE_NU * MOE_RMAX, D), f32),
        compiler_params=_cp(("arbitrary", "arbitrary")),
        name="moe_ffn",
    )(unit_e, unit_row0, unit_nt, xs, w1, w1, b1r, b1r, w2, b2r)


def _combine_kernel(pos_ref, ys_ref, x_ref, g2_ref, tw_ref, o_ref, buf_ref, sem):
    def issue(i, carry):
        for k in range(TOPK):
            p = pos_ref[0, 0, i * TOPK + k]
            pltpu.make_async_copy(ys_ref.at[pl.ds(p, 1), :], buf_ref.at[k, pl.ds(i, 1), :], sem).start()
        return carry
    lax.fori_loop(0, ROW_T, issue, 0)
    for k in range(TOPK):
        pltpu.make_async_copy(ys_ref.at[pl.ds(0, ROW_T), :], buf_ref.at[k], sem).wait()
    tw = tw_ref[...]
    y = tw[:, 0:1] * buf_ref[0]
    for k in range(1, TOPK):
        y = y + tw[:, k:k + 1] * buf_ref[k]
    o_ref[...] = x_ref[...] + g2_ref[...] * y


def moe_combine(ys, ypos, x, mod, top_w):
    nt = T // ROW_T
    return pl.pallas_call(
        _combine_kernel,
        grid=(nt,),
        in_specs=[
            pl.BlockSpec((1, 1, ROW_T * TOPK), lambda i: (i, 0, 0), memory_space=pltpu.SMEM),
            pl.BlockSpec(memory_space=pl.ANY),
            pl.BlockSpec((ROW_T, D), lambda i: (i, 0)),
            pl.BlockSpec((ROW_T, D), lambda i: (_mod_group(i), 5)),
            pl.BlockSpec((ROW_T, LANE), lambda i: (i, 0)),
        ],
        out_specs=pl.BlockSpec((ROW_T, D), lambda i: (i, 0)),
        out_shape=jax.ShapeDtypeStruct((T, D), f32),
        scratch_shapes=[pltpu.VMEM((TOPK, ROW_T, D), f32), pltpu.SemaphoreType.DMA(())],
        compiler_params=_cp(("arbitrary",)),
        name="moe_combine",
    )(ypos.reshape(nt, 1, ROW_T * TOPK), ys, x, mod, top_w)


def _pad_lanes(w):
    return jnp.pad(w, ((0, 0), (0, LANE - w.shape[1])))


def kernel(x_prompt, x_sample, state_s5_re, state_s5_im, cache_sconv, cache_conformer, cache_ssd_conv, state_ssd,
           c_prompt, c_sample, w_ada, b_ada, norm_mix, norm_ffn, w_in,
           s5_lambda_re, s5_lambda_im, s5_log_step, s5_b_re, s5_b_im, s5_c_re, s5_c_im, s5_d, s5_w_glu, w_out_a,
           sconv_w, w_out_b, conf_w, conf_b, conf_ln_g, conf_ln_b, w_out_c,
           ssd_conv_w, ssd_conv_b, ssd_dt_bias, ssd_a_log, ssd_d, ssd_norm, w_out_d, w_o,
           router_w, router_b, moe_w1, moe_b1, moe_w2, moe_b2, norm_final):
    x = jnp.concatenate([x_prompt.reshape(TP, D), x_sample.reshape(NS, D)], axis=0)
    c_all = jnp.concatenate([c_prompt, jnp.zeros((16 - NB, D), f32), c_sample], axis=0)
    srow = TP // NS
    outs_p = [[] for _ in range(6)]
    outs_s = [[] for _ in range(6)]
    for l in range(DEPTH):
        mod_c = ada_mod(c_all, w_ada, b_ada, l)
        mod = jnp.concatenate([jnp.repeat(mod_c[:NB], ROW_T, axis=0), mod_c[16:]], axis=0)
        w_dt = _pad_lanes(w_in[l, :, NMIX:NMIX + NDT]).astype(bf16)
        w_gate = w_in[l, :, NMIX + NDT:].astype(bf16)
        h, dt_raw = normmod_mix(x, norm_mix[l][None], mod, w_dt)
        proj = in_proj(h, w_in, l)
        gates = gate_proj(h, w_gate)

        a_re, a_im, bb_re, bb_im = s5_params(s5_lambda_re[l], s5_lambda_im[l], s5_log_step[l], s5_b_re[l], s5_b_im[l])
        bsup, csup = _s5_slab_weights(bb_re, bb_im, s5_c_re[l], s5_c_im[l])
        d_a = s5_d[l][None]
        ya_p, s5r_p, s5i_p = s5_seq(proj, bsup, csup, a_re, a_im, d_a)
        ya_s, s5r_s, s5i_s = s5_step(proj, srow, state_s5_re[l].reshape(NS, S5G * S5P),
                                     state_s5_im[l].reshape(NS, S5G * S5P), bsup, csup, a_re, a_im, d_a)
        y_a = s5_glu(jnp.concatenate([ya_p, ya_s], axis=0), s5_w_glu, l)

        yb_p, sconv_p = sconv_seq(proj, sconv_w[l])
        yb_s, p_s = sconv_step(proj, srow, cache_sconv[l][:, 0], cache_sconv[l][:, 1], sconv_w[l])
        y_b = jnp.concatenate([yb_p, yb_s.astype(bf16)], axis=0)
        sconv_s = jnp.stack([cache_sconv[l][:, 1], p_s], axis=1)

        cvec = (conf_b[l][None], conf_ln_g[l][None], conf_ln_b[l][None])
        yc_p, conf_p = conf_seq(proj, conf_w[l], *cvec)
        yc_s, glu_s = conf_step(proj, TP, cache_conformer[l], conf_w[l], *cvec)
        y_c = jnp.concatenate([yc_p, yc_s.astype(bf16)], axis=0)
        conf_s = jnp.concatenate([cache_conformer[l][:, 1:], glu_s[:, None]], axis=1)

        ssd_vec = (ssd_conv_w[l], ssd_conv_b[l][None], _pad_lanes(ssd_dt_bias[l][None]), _pad_lanes(ssd_a_log[l][None]),
                   jnp.repeat(ssd_d[l], SSD_P)[None], ssd_norm[l][None])
        yd_p, ssdc_p, ssdh_p = ssd_seq(proj, dt_raw, *ssd_vec)
        yd_s, ssdh_s = ssd_step(proj, dt_raw, TP, jnp.transpose(cache_ssd_conv[l], (1, 0, 2)), state_ssd[l], *ssd_vec)
        y_d = jnp.concatenate([yd_p, yd_s.astype(bf16)], axis=0)
        ssdc_s = jnp.concatenate([cache_ssd_conv[l][:, 1:], proj[TP:, 7 * DBR:9 * DBR][:, None]], axis=1)

        merged = merge_branches((y_a, y_b, y_c, y_d), (w_out_a, w_out_b, w_out_c, w_out_d), gates, l)
        o = out_proj(merged, w_o, l)

        w_r = _pad_lanes(router_w[l]).astype(bf16)
        b_r = _pad_lanes(router_b[l][None])
        x1, h2, top_i, top_w = normmod_ffn(x, o, norm_ffn[l][None], mod, w_r, b_r)
        tok_sorted, ypos, unit_e, unit_row0, unit_nt = moe_plan(top_i[:, :TOPK])
        xs = moe_gather(h2, tok_sorted)
        ys = moe_ffn(xs, unit_e, unit_row0, unit_nt, moe_w1, moe_b1, moe_w2, moe_b2, l)
        x = moe_combine(ys, ypos, x1, mod, top_w)

        for lst, v in zip(outs_p, (s5r_p, s5i_p, sconv_p, conf_p, ssdc_p, ssdh_p)):
            lst.append(v)
        for lst, v in zip(outs_s, (s5r_s.reshape(NS, S5G, S5P), s5i_s.reshape(NS, S5G, S5P), sconv_s, conf_s, ssdc_s, ssdh_s)):
            lst.append(v)

    y = final_norm(x, norm_final[None])
    p = [jnp.stack(a, axis=0) for a in outs_p]
    s = [jnp.stack(a, axis=0) for a in outs_s]
    return (y[:TP].reshape(NB, L, D), y[TP:].reshape(NS, 1, D),
            p[0], p[1], p[2], p[3], p[4], p[5], s[0], s[1], s[2], s[3], s[4], s[5])
```

```python
import functools

import jax
import jax.numpy as jnp
import numpy as np
from jax import lax
from jax.experimental import pallas as pl
from jax.experimental.pallas import tpu as pltpu

f32 = jnp.float32
bf16 = jnp.bfloat16
i32 = jnp.int32

D = 4096
NB = 4
L = 2048
NS = 128
TP = NB * L
T = TP + NS
DEPTH = 2
DBR = 1024
NMIX = 9 * DBR
NDT = 16
NGATE = 4 * D
S5G = 64
S5C = 16
S5P = 64
SLAB_G = 16
NSLAB = S5G // SLAB_G
SLAB_C = SLAB_G * S5C
SLAB_P = SLAB_G * S5P
NCHUNK = 8
CONF_W = 31
CONF_TT = 256
SSD_H = 16
SSD_P = 64
SSD_G = 4
SSD_N = 128
SSD_Q = 128
NE = 32
TOPK = 4
DFF = 2048
EPS = 1e-5
LANE = 128
ROW_T = 128
BM = 1040
TN = 512
MOE_PAD = 128
MOE_GT = 256
MOE_RMAX = 1152
MOE_NP = T * TOPK + NE * MOE_PAD + 5 * MOE_GT
MOE_NU = 64
MOE_NU_SMALL = 36
MOE_TF = 256
MOE_TO = 512
N_P1 = DFF // MOE_TF
N_P2 = D // MOE_TO
VMEM_LIMIT = 56 * 1024 * 1024


def _cp(sem, vmem=VMEM_LIMIT):
    return pltpu.CompilerParams(dimension_semantics=sem, vmem_limit_bytes=vmem)


def _dot(a, b):
    return jnp.dot(a, b, preferred_element_type=f32)


def _dot_tt(a, b):
    return lax.dot_general(a, b, (((0,), (0,)), ((), ())), preferred_element_type=f32)


def _dot_nt(a, b):
    return lax.dot_general(a, b, (((1,), (1,)), ((), ())), preferred_element_type=f32)


def _sigmoid(x):
    return 1.0 / (1.0 + jnp.exp(-x))


def _silu(x):
    return x * _sigmoid(x)


def _softplus(x):
    return jnp.maximum(x, 0.0) + jnp.log1p(jnp.exp(-jnp.abs(x)))


def _ada_kernel(c_ref, w_ref, b_ref, o_ref):
    a = _silu(c_ref[...]).astype(bf16)
    o_ref[...] = _dot(a, w_ref[...].astype(bf16)) + b_ref[...]


def ada_mod(c_all, w_ada, b_ada, l):
    m = c_all.shape[0]
    n = 6 * D
    return pl.pallas_call(
        _ada_kernel,
        grid=(n // TN,),
        in_specs=[
            pl.BlockSpec((m, D), lambda j: (0, 0)),
            pl.BlockSpec((None, D, TN), lambda j: (l, 0, j)),
            pl.BlockSpec((None, 1, TN), lambda j: (l, 0, j)),
        ],
        out_specs=pl.BlockSpec((m, TN), lambda j: (0, j)),
        out_shape=jax.ShapeDtypeStruct((m, n), f32),
        compiler_params=_cp(("arbitrary",)),
        name="ada_mod",
    )(c_all, w_ada, b_ada.reshape(DEPTH, 1, n))


def _mm_f32w_kernel(a_ref, w_ref, o_ref, wbf_ref):
    @pl.when(pl.program_id(1) == 0)
    def _():
        wbf_ref[...] = w_ref[...].astype(bf16)
    o_ref[...] = _dot(a_ref[...], wbf_ref[...]).astype(o_ref.dtype)


def in_proj(h, w_in, l):
    k = h.shape[1]
    return pl.pallas_call(
        _mm_f32w_kernel,
        grid=(NMIX // TN, T // BM),
        in_specs=[
            pl.BlockSpec((BM, k), lambda j, i: (i, 0)),
            pl.BlockSpec((None, k, TN), lambda j, i: (l, 0, j)),
        ],
        out_specs=pl.BlockSpec((BM, TN), lambda j, i: (i, j)),
        out_shape=jax.ShapeDtypeStruct((T, NMIX), f32),
        scratch_shapes=[pltpu.VMEM((k, TN), bf16)],
        compiler_params=_cp(("arbitrary", "arbitrary")),
        name="in_proj",
    )(h, w_in)


def _gate_kernel(a_ref, w_ref, wx_ref, o_ref, wbf_ref):
    @pl.when(pl.program_id(1) == 0)
    def _():
        w = jnp.concatenate([w_ref[...], wx_ref[...]], axis=1)
        wbf_ref[...] = w[:, NDT:NDT + TN].astype(bf16)
    o_ref[...] = _sigmoid(_dot(a_ref[...], wbf_ref[...])).astype(o_ref.dtype)


def gate_proj(h, w_in, l):
    k = h.shape[1]
    j0 = NMIX // TN
    return pl.pallas_call(
        _gate_kernel,
        grid=(NGATE // TN, T // BM),
        in_specs=[
            pl.BlockSpec((BM, k), lambda j, i: (i, 0)),
            pl.BlockSpec((None, k, TN), lambda j, i: (l, 0, j0 + j)),
            pl.BlockSpec((None, k, LANE), lambda j, i: (l, 0, (j0 + j + 1) * (TN // LANE))),
        ],
        out_specs=pl.BlockSpec((BM, TN), lambda j, i: (i, j)),
        out_shape=jax.ShapeDtypeStruct((T, NGATE), bf16),
        scratch_shapes=[pltpu.VMEM((k, TN), bf16)],
        compiler_params=_cp(("arbitrary", "arbitrary")),
        name="gate_proj",
    )(h, w_in, w_in)


def _merge_kernel(ya_ref, yb_ref, yc_ref, yd_ref, wa_ref, wb_ref, wc_ref, wd_ref,
                  ga_ref, gb_ref, gc_ref, gd_ref, o_ref, wbf_ref):
    @pl.when(pl.program_id(1) == 0)
    def _():
        for i, w_ref in enumerate((wa_ref, wb_ref, wc_ref, wd_ref)):
            wbf_ref[i] = w_ref[...].astype(bf16)
    acc = None
    for i, (y_ref, g_ref) in enumerate(((ya_ref, ga_ref), (yb_ref, gb_ref), (yc_ref, gc_ref), (yd_ref, gd_ref))):
        term = g_ref[...].astype(f32) * _dot(y_ref[...], wbf_ref[i])
        acc = term if acc is None else acc + term
    o_ref[...] = acc.astype(o_ref.dtype)


def merge_branches(ys, w_outs, gates, l):
    nt = D // TN
    y_specs = [pl.BlockSpec((BM, DBR), lambda j, i: (i, 0)) for _ in range(4)]
    w_specs = [pl.BlockSpec((None, DBR, TN), lambda j, i: (l, 0, j)) for _ in range(4)]
    g_specs = [pl.BlockSpec((BM, TN), functools.partial(lambda j, i, br: (i, br * nt + j), br=br)) for br in range(4)]
    return pl.pallas_call(
        _merge_kernel,
        grid=(nt, T // BM),
        in_specs=y_specs + w_specs + g_specs,
        out_specs=pl.BlockSpec((BM, TN), lambda j, i: (i, j)),
        out_shape=jax.ShapeDtypeStruct((T, D), bf16),
        scratch_shapes=[pltpu.VMEM((4, DBR, TN), bf16)],
        compiler_params=_cp(("arbitrary", "arbitrary")),
        name="merge_branches",
    )(*ys, *w_outs, gates, gates, gates, gates)


def out_proj(merged, w_o, l):
    return pl.pallas_call(
        _mm_f32w_kernel,
        grid=(D // TN, T // BM),
        in_specs=[
            pl.BlockSpec((BM, D), lambda j, i: (i, 0)),
            pl.BlockSpec((None, D, TN), lambda j, i: (l, 0, j)),
        ],
        out_specs=pl.BlockSpec((BM, TN), lambda j, i: (i, j)),
        out_shape=jax.ShapeDtypeStruct((T, D), f32),
        scratch_shapes=[pltpu.VMEM((D, TN), bf16)],
        compiler_params=_cp(("arbitrary", "arbitrary")),
        name="out_proj",
    )(merged, w_o)


def _mod_group(i):
    return jnp.minimum(i // (L // ROW_T), NB)


def _rms_mod(x, g, scale, shift):
    y = x * lax.rsqrt(jnp.mean(x * x, axis=-1, keepdims=True) + EPS) * g
    return y * (1.0 + scale) + shift


def _normmod1_kernel(x_ref, g_ref, sc_ref, sh_ref, wdt_ref, h_ref, dt_ref):
    hb = _rms_mod(x_ref[...], g_ref[...], sc_ref[...], sh_ref[...]).astype(bf16)
    h_ref[...] = hb
    dt_ref[...] = _dot(hb, wdt_ref[...])


def normmod_mix(x, g, mod, w_dt):
    return pl.pallas_call(
        _normmod1_kernel,
        grid=(T // ROW_T,),
        in_specs=[
            pl.BlockSpec((ROW_T, D), lambda i: (i, 0)),
            pl.BlockSpec((1, D), lambda i: (0, 0)),
            pl.BlockSpec((ROW_T, D), lambda i: (_mod_group(i), 1)),
            pl.BlockSpec((ROW_T, D), lambda i: (_mod_group(i), 0)),
            pl.BlockSpec((D, LANE), lambda i: (0, 0)),
        ],
        out_specs=[
            pl.BlockSpec((ROW_T, D), lambda i: (i, 0)),
            pl.BlockSpec((ROW_T, LANE), lambda i: (i, 0)),
        ],
        out_shape=[jax.ShapeDtypeStruct((T, D), bf16), jax.ShapeDtypeStruct((T, LANE), f32)],
        compiler_params=_cp(("arbitrary",)),
        name="normmod_mix",
    )(x, g, mod, mod, w_dt)


def _normmod2_kernel(x_ref, o_ref, g1_ref, g_ref, sc_ref, sh_ref, wrh_ref, wrl_ref, br_ref,
                     x1_ref, h_ref, ti_ref, tw_ref):
    x1 = x_ref[...] + g1_ref[...] * o_ref[...]
    x1_ref[...] = x1
    h = _rms_mod(x1, g_ref[...], sc_ref[...], sh_ref[...])
    h_ref[...] = h.reshape(ROW_T, D // LANE, LANE)
    hh = h.astype(bf16)
    hl = (h - hh.astype(f32)).astype(bf16)
    logits = _dot(hh, wrh_ref[...]) + (_dot(hl, wrh_ref[...]) + _dot(hh, wrl_ref[...])) + br_ref[...]
    lane = lax.broadcasted_iota(i32, logits.shape, 1)
    logits = jnp.where(lane < NE, logits, -jnp.inf)
    ti = jnp.zeros(logits.shape, i32)
    tv = jnp.zeros(logits.shape, f32)
    for k in range(TOPK):
        m = jnp.max(logits, axis=-1, keepdims=True)
        idx = jnp.min(jnp.where(logits == m, lane, LANE), axis=-1, keepdims=True)
        ti = jnp.where(lane == k, idx, ti)
        tv = jnp.where(lane == k, m, tv)
        logits = jnp.where(lane == idx, -jnp.inf, logits)
    ex = jnp.where(lane < TOPK, jnp.exp(tv - tv[:, 0:1]), 0.0)
    ti_ref[...] = ti
    tw_ref[...] = ex / jnp.sum(ex, axis=-1, keepdims=True)


def normmod_ffn(x, o, g, mod, w_router_hi, w_router_lo, b_router):
    return pl.pallas_call(
        _normmod2_kernel,
        grid=(T // ROW_T,),
        in_specs=[
            pl.BlockSpec((ROW_T, D), lambda i: (i, 0)),
            pl.BlockSpec((ROW_T, D), lambda i: (i, 0)),
            pl.BlockSpec((ROW_T, D), lambda i: (_mod_group(i), 2)),
            pl.BlockSpec((1, D), lambda i: (0, 0)),
            pl.BlockSpec((ROW_T, D), lambda i: (_mod_group(i), 4)),
            pl.BlockSpec((ROW_T, D), lambda i: (_mod_group(i), 3)),
            pl.BlockSpec((D, LANE), lambda i: (0, 0)),
            pl.BlockSpec((D, LANE), lambda i: (0, 0)),
            pl.BlockSpec((1, LANE), lambda i: (0, 0)),
        ],
        out_specs=[
            pl.BlockSpec((ROW_T, D), lambda i: (i, 0)),
            pl.BlockSpec((ROW_T, D // LANE, LANE), lambda i: (i, 0, 0)),
            pl.BlockSpec((ROW_T, LANE), lambda i: (i, 0)),
            pl.BlockSpec((ROW_T, LANE), lambda i: (i, 0)),
        ],
        out_shape=[jax.ShapeDtypeStruct((T, D), f32), jax.ShapeDtypeStruct((T, D // LANE, LANE), f32),
                   jax.ShapeDtypeStruct((T, LANE), i32), jax.ShapeDtypeStruct((T, LANE), f32)],
        compiler_params=_cp(("arbitrary",)),
        name="normmod_ffn",
    )(x, o, mod, g, mod, mod, w_router_hi, w_router_lo, b_router)


def _final_norm_kernel(x_ref, g_ref, o_ref):
    x = x_ref[...]
    o_ref[...] = x * lax.rsqrt(jnp.mean(x * x, axis=-1, keepdims=True) + EPS) * g_ref[...]


def final_norm(x, g):
    return pl.pallas_call(
        _final_norm_kernel,
        grid=(T // ROW_T,),
        in_specs=[pl.BlockSpec((ROW_T, D), lambda i: (i, 0)), pl.BlockSpec((1, D), lambda i: (0, 0))],
        out_specs=pl.BlockSpec((ROW_T, D), lambda i: (i, 0)),
        out_shape=jax.ShapeDtypeStruct((T, D), f32),
        compiler_params=_cp(("arbitrary",)),
        name="final_norm",
    )(x, g)


def _s5_param_kernel(lr_ref, li_ref, ls_ref, br_ref, bi_ref, ar_ref, ai_ref, bbr_ref, bbi_ref):
    lr = lr_ref[...]
    li = li_ref[...]
    step = jnp.exp(ls_ref[...])
    mag = jnp.exp(lr * step)
    a_re = mag * jnp.cos(li * step)
    a_im = mag * jnp.sin(li * step)
    den = lr * lr + li * li
    q_re = ((a_re - 1.0) * lr + a_im * li) / den
    q_im = (a_im * lr - (a_re - 1.0) * li) / den
    ar_ref[...] = a_re
    ai_ref[...] = a_im
    for i in range(S5C):
        br = br_ref[i]
        bi = bi_ref[i]
        bbr_ref[i] = q_re * br - q_im * bi
        bbi_ref[i] = q_re * bi + q_im * br


def s5_params(lam_re, lam_im, log_step, b_re, b_im):
    b_re_t = jnp.transpose(b_re, (2, 0, 1))
    b_im_t = jnp.transpose(b_im, (2, 0, 1))
    gp = jax.ShapeDtypeStruct((S5G, S5P), f32)
    cgp = jax.ShapeDtypeStruct((S5C, S5G, S5P), f32)
    return pl.pallas_call(_s5_param_kernel, out_shape=[gp, gp, cgp, cgp], name="s5_params")(
        lam_re, lam_im, log_step.reshape(S5G, 1), b_re_t, b_im_t)


def _s5_slab_weights(bb_re, bb_im, c_re, c_im):
    eye = jnp.eye(SLAB_G, dtype=f32)

    def b_blocks(bb):
        t = bb.reshape(S5C, NSLAB, SLAB_G, S5P)
        return jnp.einsum('isgp,gh->sgihp', t, eye).reshape(NSLAB, SLAB_C, SLAB_P)

    def c_blocks(c):
        t = c.reshape(NSLAB, SLAB_G, S5C, S5P)
        return jnp.einsum('sgip,gh->sgphi', t, eye).reshape(NSLAB, SLAB_P, SLAB_C)

    bsup = jnp.concatenate([b_blocks(bb_re), b_blocks(bb_im)], axis=2).astype(bf16)
    csup = jnp.concatenate([c_blocks(c_re), -c_blocks(c_im)], axis=1).astype(bf16)
    return bsup, csup


def _cmul_add(ar, ai, hr, hi, br, bi):
    return ar * hr - ai * hi + br, ar * hi + ai * hr + bi


def _s5_seq_kernel(u_ref, bsup_ref, csup_ref, ar_ref, ai_ref, d_ref, y_ref, sr_ref, si_ref, scr_ref, *, seq):
    q_len = seq // NCHUNK
    nj = SLAB_P // LANE
    up = pltpu.einshape("cqd->qcd", u_ref[...].reshape(NCHUNK, q_len, SLAB_C)).reshape(seq, SLAB_C)
    upb = up.astype(bf16)
    for k in range(nj):
        bu = _dot(upb, bsup_ref[:, 2 * LANE * k:2 * LANE * (k + 1)])
        scr_ref[2 * k] = bu[:, :LANE]
        scr_ref[2 * k + 1] = bu[:, LANE:]
    ar = [jnp.broadcast_to(ar_ref[:, LANE * j:LANE * (j + 1)], (NCHUNK, LANE)) for j in range(nj)]
    ai = [jnp.broadcast_to(ai_ref[:, LANE * j:LANE * (j + 1)], (NCHUNK, LANE)) for j in range(nj)]

    def scan(init, store):
        def body(q, hs):
            rows = pl.ds(pl.multiple_of(q * NCHUNK, NCHUNK), NCHUNK)
            out_r, out_i = [], []
            for j in range(nj):
                nr, ni = _cmul_add(ar[j], ai[j], hs[j], hs[nj + j], scr_ref[j, rows, :], scr_ref[nj + j, rows, :])
                if store:
                    scr_ref[j, rows, :] = nr
                    scr_ref[nj + j, rows, :] = ni
                out_r.append(nr)
                out_i.append(ni)
            return tuple(out_r + out_i)
        return lax.fori_loop(0, q_len, body, init)

    zeros = tuple(jnp.zeros((NCHUNK, LANE), f32) for _ in range(2 * nj))
    ends = scan(zeros, store=False)
    pr = [a[0:1] for a in ar]
    pi = [a[0:1] for a in ai]
    for _ in range(int(np.log2(q_len))):
        pr, pi = ([pr[j] * pr[j] - pi[j] * pi[j] for j in range(nj)],
                  [2.0 * pr[j] * pi[j] for j in range(nj)])
    starts = []
    for j in range(nj):
        hr = jnp.zeros((1, LANE), f32)
        hi = jnp.zeros((1, LANE), f32)
        rr, ri = [hr], [hi]
        for c in range(NCHUNK - 1):
            hr, hi = _cmul_add(pr[j], pi[j], hr, hi, ends[j][c:c + 1], ends[nj + j][c:c + 1])
            rr.append(hr)
            ri.append(hi)
        starts.append((jnp.concatenate(rr, axis=0), jnp.concatenate(ri, axis=0)))
    init = tuple([s[0] for s in starts] + [s[1] for s in starts])
    fin = scan(init, store=True)
    sr_ref[...] = jnp.concatenate([fin[j][NCHUNK - 1:NCHUNK] for j in range(nj)], axis=1)
    si_ref[...] = jnp.concatenate([fin[nj + j][NCHUNK - 1:NCHUNK] for j in range(nj)], axis=1)
    acc = d_ref[...] * up
    for k in range(nj):
        hk = jnp.concatenate([scr_ref[2 * k], scr_ref[2 * k + 1]], axis=1).astype(bf16)
        acc = acc + _dot(hk, csup_ref[2 * LANE * k:2 * LANE * (k + 1), :])
    y_ref[...] = pltpu.einshape("qcd->cqd", acc.reshape(q_len, NCHUNK, SLAB_C)).reshape(seq, SLAB_C)


def s5_seq(proj, bsup, csup, a_re, a_im, d_skip, nseq=NB, seq=L):
    a_spec = pl.BlockSpec((None, 1, SLAB_P), lambda b, s: (s, 0, 0))
    st_spec = pl.BlockSpec((None, None, 1, SLAB_P), lambda b, s: (b, s, 0, 0))
    st_shape = jax.ShapeDtypeStruct((nseq, NSLAB, 1, SLAB_P), f32)
    y, sr, si = pl.pallas_call(
        functools.partial(_s5_seq_kernel, seq=seq),
        grid=(nseq, NSLAB),
        in_specs=[
            pl.BlockSpec((seq, SLAB_C), lambda b, s: (b, s)),
            pl.BlockSpec((None, SLAB_C, 2 * SLAB_P), lambda b, s: (s, 0, 0)),
            pl.BlockSpec((None, 2 * SLAB_P, SLAB_C), lambda b, s: (s, 0, 0)),
            a_spec, a_spec,
            pl.BlockSpec((1, SLAB_C), lambda b, s: (0, s)),
        ],
        out_specs=[pl.BlockSpec((seq, SLAB_C), lambda b, s: (b, s)), st_spec, st_spec],
        out_shape=[jax.ShapeDtypeStruct((nseq * seq, DBR), f32), st_shape, st_shape],
        scratch_shapes=[pltpu.VMEM((2 * SLAB_P // LANE, seq, LANE), f32)],
        compiler_params=_cp(("arbitrary", "arbitrary")),
        name="s5_seq",
    )(proj, bsup, csup, a_re.reshape(NSLAB, 1, SLAB_P), a_im.reshape(NSLAB, 1, SLAB_P), d_skip)
    return y, sr.reshape(nseq, S5G, S5P), si.reshape(nseq, S5G, S5P)


def _s5_step_kernel(u_ref, h0r_ref, h0i_ref, bsup_ref, csup_ref, ar_ref, ai_ref, d_ref, y_ref, sr_ref, si_ref):
    u = u_ref[...]
    bu = _dot(u.astype(bf16), bsup_ref[...])
    nr, ni = _cmul_add(ar_ref[...], ai_ref[...], h0r_ref[...], h0i_ref[...], bu[:, :SLAB_P], bu[:, SLAB_P:])
    sr_ref[...] = nr
    si_ref[...] = ni
    h = jnp.concatenate([nr, ni], axis=1).astype(bf16)
    y_ref[...] = _dot(h, csup_ref[...]) + d_ref[...] * u


def s5_step(proj, row_block, h0_re, h0_im, bsup, csup, a_re, a_im, d_skip):
    a_spec = pl.BlockSpec((None, 1, SLAB_P), lambda s: (s, 0, 0))
    st_spec = pl.BlockSpec((NS, SLAB_P), lambda s: (0, s))
    st_shape = jax.ShapeDtypeStruct((NS, S5G * S5P), f32)
    return pl.pallas_call(
        _s5_step_kernel,
        grid=(NSLAB,),
        in_specs=[
            pl.BlockSpec((NS, SLAB_C), lambda s: (row_block, s)),
            st_spec, st_spec,
            pl.BlockSpec((None, SLAB_C, 2 * SLAB_P), lambda s: (s, 0, 0)),
            pl.BlockSpec((None, 2 * SLAB_P, SLAB_C), lambda s: (s, 0, 0)),
            a_spec, a_spec,
            pl.BlockSpec((1, SLAB_C), lambda s: (0, s)),
        ],
        out_specs=[pl.BlockSpec((NS, SLAB_C), lambda s: (0, s)), st_spec, st_spec],
        out_shape=[jax.ShapeDtypeStruct((NS, DBR), f32), st_shape, st_shape],
        compiler_params=_cp(("arbitrary",)),
        name="s5_step",
    )(proj, h0_re, h0_im, bsup, csup, a_re.reshape(NSLAB, 1, SLAB_P), a_im.reshape(NSLAB, 1, SLAB_P), d_skip)


def _s5_glu_kernel(y_ref, w_ref, o_ref):
    g = jax.nn.gelu(y_ref[...])
    o_ref[...] = (g * _sigmoid(_dot(g.astype(bf16), w_ref[...].astype(bf16)))).astype(o_ref.dtype)


def s5_glu(y, w_glu, l):
    m = y.shape[0]
    return pl.pallas_call(
        _s5_glu_kernel,
        grid=(m // BM,),
        in_specs=[pl.BlockSpec((BM, DBR), lambda i: (i, 0)), pl.BlockSpec((None, DBR, DBR), lambda i: (l, 0, 0))],
        out_specs=pl.BlockSpec((BM, DBR), lambda i: (i, 0)),
        out_shape=jax.ShapeDtypeStruct((m, DBR), bf16),
        compiler_params=_cp(("arbitrary",)),
        name="s5_glu",
    )(y, w_glu)


SC_CT = 256
SC_HDR = 8


def _sconv_seq_kernel(bg_ref, cg_ref, v_ref, w_ref, o_ref, cache_ref, scr_ref, *, seq):
    p = cg_ref[...] * v_ref[...]
    scr_ref[pl.ds(0, SC_HDR), :] = jnp.zeros((SC_HDR, SC_CT), f32)
    scr_ref[pl.ds(SC_HDR, seq), :] = p
    conv = (w_ref[0:1, :] * scr_ref[pl.ds(SC_HDR - 2, seq), :]
            + w_ref[1:2, :] * scr_ref[pl.ds(SC_HDR - 1, seq), :]
            + w_ref[2:3, :] * p)
    o_ref[...] = (bg_ref[...] * conv).astype(o_ref.dtype)
    cache_ref[...] = scr_ref[pl.ds(SC_HDR + seq - 2, 2), :]


def sconv_seq(proj, w, nseq=NB, seq=L):
    nct = DBR // SC_CT
    return pl.pallas_call(
        functools.partial(_sconv_seq_kernel, seq=seq),
        grid=(nseq, nct),
        in_specs=[
            pl.BlockSpec((seq, SC_CT), lambda b, c: (b, nct + c)),
            pl.BlockSpec((seq, SC_CT), lambda b, c: (b, 2 * nct + c)),
            pl.BlockSpec((seq, SC_CT), lambda b, c: (b, 3 * nct + c)),
            pl.BlockSpec((3, SC_CT), lambda b, c: (0, c)),
        ],
        out_specs=[pl.BlockSpec((seq, SC_CT), lambda b, c: (b, c)),
                   pl.BlockSpec((None, 2, SC_CT), lambda b, c: (b, 0, c))],
        out_shape=[jax.ShapeDtypeStruct((nseq * seq, DBR), bf16), jax.ShapeDtypeStruct((nseq, 2, DBR), f32)],
        scratch_shapes=[pltpu.VMEM((SC_HDR + seq, SC_CT), f32)],
        compiler_params=_cp(("arbitrary", "arbitrary")),
        name="sconv_seq",
    )(proj, proj, proj, w)


def _sconv_step_kernel(bg_ref, cg_ref, v_ref, c0_ref, c1_ref, w_ref, o_ref, p_ref):
    p = cg_ref[...] * v_ref[...]
    p_ref[...] = p
    conv = w_ref[0:1, :] * c0_ref[...] + w_ref[1:2, :] * c1_ref[...] + w_ref[2:3, :] * p
    o_ref[...] = (bg_ref[...] * conv).astype(o_ref.dtype)


def sconv_step(proj, row_block, cache0, cache1, w):
    blk = lambda col: pl.BlockSpec((NS, DBR), lambda i: (row_block, col))
    full = pl.BlockSpec((NS, DBR), lambda i: (0, 0))
    return pl.pallas_call(
        _sconv_step_kernel,
        grid=(1,),
        in_specs=[blk(1), blk(2), blk(3), full, full, pl.BlockSpec((3, DBR), lambda i: (0, 0))],
        out_specs=[full, full],
        out_shape=[jax.ShapeDtypeStruct((NS, DBR), f32), jax.ShapeDtypeStruct((NS, DBR), f32)],
        compiler_params=_cp(("arbitrary",)),
        name="sconv_step",
    )(proj, proj, proj, cache0, cache1, w)


CF_HDR = 32


def _layernorm_silu(x, g, b):
    xc = x - jnp.mean(x, axis=-1, keepdims=True)
    var = jnp.mean(xc * xc, axis=-1, keepdims=True)
    return _silu(xc * lax.rsqrt(var + EPS) * g + b)


def _conf_seq_kernel(v_ref, g_ref, w_ref, cb_ref, lg_ref, lb_ref, o_ref, cache_ref, win_ref, *, n_tiles):
    t = pl.program_id(1)

    @pl.when(t == 0)
    def _():
        win_ref[pl.ds(0, CF_HDR), :] = jnp.zeros((CF_HDR, DBR), f32)

    glu = v_ref[...] * _sigmoid(g_ref[...])
    win_ref[pl.ds(CF_HDR, CONF_TT), :] = glu
    acc = w_ref[CONF_W - 1:CONF_W, :] * glu
    for k in range(CONF_W - 1):
        acc = acc + w_ref[k:k + 1, :] * win_ref[pl.ds(CF_HDR - (CONF_W - 1) + k, CONF_TT), :]
    o_ref[...] = _layernorm_silu(acc + cb_ref[...], lg_ref[...], lb_ref[...]).astype(o_ref.dtype)

    @pl.when(t == n_tiles - 1)
    def _():
        cache_ref[...] = win_ref[pl.ds(CF_HDR + CONF_TT - (CONF_W - 1), CONF_W - 1), :]

    win_ref[pl.ds(0, CF_HDR), :] = win_ref[pl.ds(CONF_TT, CF_HDR), :]


def conf_seq(proj, w, cb, lg, lb, nseq=NB, seq=L):
    n_tiles = seq // CONF_TT
    vec = pl.BlockSpec((1, DBR), lambda b, t: (0, 0))
    return pl.pallas_call(
        functools.partial(_conf_seq_kernel, n_tiles=n_tiles),
        grid=(nseq, n_tiles),
        in_specs=[
            pl.BlockSpec((CONF_TT, DBR), lambda b, t: (b * n_tiles + t, 4)),
            pl.BlockSpec((CONF_TT, DBR), lambda b, t: (b * n_tiles + t, 5)),
            pl.BlockSpec((CONF_W, DBR), lambda b, t: (0, 0)),
            vec, vec, vec,
        ],
        out_specs=[pl.BlockSpec((CONF_TT, DBR), lambda b, t: (b * n_tiles + t, 0)),
                   pl.BlockSpec((None, CONF_W - 1, DBR), lambda b, t: (b, 0, 0))],
        out_shape=[jax.ShapeDtypeStruct((nseq * seq, DBR), bf16),
                   jax.ShapeDtypeStruct((nseq, CONF_W - 1, DBR), f32)],
        scratch_shapes=[pltpu.VMEM((CF_HDR + CONF_TT, DBR), f32)],
        compiler_params=_cp(("arbitrary", "arbitrary")),
        name="conf_seq",
    )(proj, proj, w, cb, lg, lb)


CF_BT = 16


def _conf_step_kernel(v_ref, g_ref, c_ref, w_ref, cb_ref, lg_ref, lb_ref, o_ref, glu_ref):
    glu = v_ref[...] * _sigmoid(g_ref[...])
    glu_ref[...] = glu
    conv = jnp.sum(c_ref[...] * w_ref[0:CONF_W - 1, :][None], axis=1) + w_ref[CONF_W - 1:CONF_W, :] * glu
    o_ref[...] = _layernorm_silu(conv + cb_ref[...], lg_ref[...], lb_ref[...]).astype(o_ref.dtype)


def conf_step(proj, row0, cache, w, cb, lg, lb):
    rb = row0 // CF_BT
    vec = pl.BlockSpec((1, DBR), lambda i: (0, 0))
    row = pl.BlockSpec((CF_BT, DBR), lambda i: (i, 0))
    return pl.pallas_call(
        _conf_step_kernel,
        grid=(NS // CF_BT,),
        in_specs=[
            pl.BlockSpec((CF_BT, DBR), lambda i: (rb + i, 4)),
            pl.BlockSpec((CF_BT, DBR), lambda i: (rb + i, 5)),
            pl.BlockSpec((CF_BT, CONF_W - 1, DBR), lambda i: (i, 0, 0)),
            pl.BlockSpec((CONF_W, DBR), lambda i: (0, 0)),
            vec, vec, vec,
        ],
        out_specs=[row, row],
        out_shape=[jax.ShapeDtypeStruct((NS, DBR), f32), jax.ShapeDtypeStruct((NS, DBR), f32)],
        compiler_params=_cp(("arbitrary",)),
        name="conf_step",
    )(proj, proj, cache, w, cb, lg, lb)


SSD_HDR = 8
SSD_CW = 4
SSD_GC = DBR // SSD_G


def _split3(x):
    hi = x.astype(bf16)
    r1 = x - hi.astype(f32)
    mid = r1.astype(bf16)
    lo = (r1 - mid.astype(f32)).astype(bf16)
    return hi, mid, lo


def _gated_group_norm(y, z, g):
    yz = y * _silu(z)
    outs = []
    for gi in range(SSD_G):
        seg = yz[:, SSD_GC * gi:SSD_GC * (gi + 1)]
        outs.append(seg * lax.rsqrt(jnp.mean(seg * seg, axis=-1, keepdims=True) + EPS))
    return jnp.concatenate(outs, axis=1) * g


def _ssd_seq_kernel(z_ref, x_ref, bc_ref, dt_ref, wx_ref, wbc_ref, bx_ref, bbc_ref, dtb_ref, alog_ref,
                    dch_ref, ng_ref, y_ref, cache_ref, hout_ref, xwin_ref, bcwin_ref, h_ref, *, n_chunks):
    c = pl.program_id(1)
    q = SSD_Q

    @pl.when(c == 0)
    def _():
        xwin_ref[pl.ds(0, SSD_HDR), :] = jnp.zeros((SSD_HDR, DBR), f32)
        bcwin_ref[pl.ds(0, SSD_HDR), :] = jnp.zeros((SSD_HDR, DBR), f32)
        h_ref[...] = jnp.zeros(h_ref.shape, f32)

    def conv(raw_ref, win_ref, w_ref, b_ref):
        raw = raw_ref[...]
        win_ref[pl.ds(SSD_HDR, q), :] = raw
        acc = w_ref[SSD_CW - 1:SSD_CW, :] * raw + b_ref[...]
        for k in range(SSD_CW - 1):
            acc = acc + w_ref[k:k + 1, :] * win_ref[pl.ds(SSD_HDR - (SSD_CW - 1) + k, q), :]
        return _silu(acc)

    xs = conv(x_ref, xwin_ref, wx_ref, bx_ref)
    bc = conv(bc_ref, bcwin_ref, wbc_ref, bbc_ref)

    @pl.when(c == n_chunks - 1)
    def _():
        cache_ref[:, 0:DBR] = xwin_ref[pl.ds(SSD_HDR + q - (SSD_CW - 1), SSD_CW - 1), :]
        cache_ref[:, DBR:2 * DBR] = bcwin_ref[pl.ds(SSD_HDR + q - (SSD_CW - 1), SSD_CW - 1), :]

    xwin_ref[pl.ds(0, SSD_HDR), :] = xwin_ref[pl.ds(q, SSD_HDR), :]
    bcwin_ref[pl.ds(0, SSD_HDR), :] = bcwin_ref[pl.ds(q, SSD_HDR), :]

    lane = lax.broadcasted_iota(i32, (q, LANE), 1)
    dt = jnp.where(lane < SSD_H, _softplus(dt_ref[...] + dtb_ref[...]), 0.0)
    a = -jnp.exp(alog_ref[...])
    dta = jnp.where(lane < SSD_H, dt * a, 0.0)
    ri = lax.broadcasted_iota(i32, (q, q), 0)
    ci = lax.broadcasted_iota(i32, (q, q), 1)
    causal = ri >= ci
    tri = causal.astype(bf16)
    parts = _split3(dta)
    cs = sum(_dot(tri, p) for p in parts)
    cst = sum(_dot_tt(p, (ri <= ci).astype(bf16)) for p in parts)

    y_parts = []
    for g in range(SSD_G):
        bg = bc[:, SSD_N * g:SSD_N * (g + 1)].astype(bf16)
        cg = bc[:, SSD_G * SSD_N + SSD_N * g:SSD_G * SSD_N + SSD_N * (g + 1)].astype(bf16)
        scores = _dot_nt(cg, bg)
        for r in range(SSD_H // SSD_G):
            h = g * (SSD_H // SSD_G) + r
            col = cs[:, h:h + 1]
            row = cst[h:h + 1, :]
            decay = jnp.exp(jnp.where(causal, col - row, -jnp.inf))
            xh = xs[:, SSD_P * h:SSD_P * (h + 1)]
            xdt = xh * dt[:, h:h + 1]
            y_intra = _dot((scores * decay).astype(bf16), xdt.astype(bf16))
            hprev = h_ref[h]
            y_inter = _dot_nt(cg, hprev.astype(bf16)) * jnp.exp(col)
            last = cs[q - 1:q, h:h + 1]
            to_end = jnp.exp(last - col)
            h_ref[h] = jnp.exp(last) * hprev + _dot_tt((xdt * to_end).astype(bf16), bg)
            y_parts.append(y_intra + y_inter)
    y = jnp.concatenate(y_parts, axis=1) + dch_ref[...] * xs
    y_ref[...] = _gated_group_norm(y, z_ref[...], ng_ref[...]).astype(y_ref.dtype)

    @pl.when(c == n_chunks - 1)
    def _():
        hout_ref[...] = h_ref[...]


def ssd_seq(proj, dt_raw, conv_w, conv_b, dt_bias, a_log, d_ch, norm_g, nseq=NB, seq=L):
    n_chunks = seq // SSD_Q
    rowblk = lambda col: pl.BlockSpec((SSD_Q, DBR), lambda b, c: (b * n_chunks + c, col))
    vec = lambda col: pl.BlockSpec((1, DBR), lambda b, c: (0, col))
    vecl = pl.BlockSpec((1, LANE), lambda b, c: (0, 0))
    return pl.pallas_call(
        functools.partial(_ssd_seq_kernel, n_chunks=n_chunks),
        grid=(nseq, n_chunks),
        in_specs=[
            rowblk(6), rowblk(7), rowblk(8),
            pl.BlockSpec((SSD_Q, LANE), lambda b, c: (b * n_chunks + c, 0)),
            pl.BlockSpec((SSD_CW, DBR), lambda b, c: (0, 0)),
            pl.BlockSpec((SSD_CW, DBR), lambda b, c: (0, 1)),
            vec(0), vec(1), vecl, vecl, vec(0), vec(0),
        ],
        out_specs=[
            pl.BlockSpec((SSD_Q, DBR), lambda b, c: (b * n_chunks + c, 0)),
            pl.BlockSpec((None, SSD_CW - 1, 2 * DBR), lambda b, c: (b, 0, 0)),
            pl.BlockSpec((None, SSD_H, SSD_P, SSD_N), lambda b, c: (b, 0, 0, 0)),
        ],
        out_shape=[jax.ShapeDtypeStruct((nseq * seq, DBR), bf16),
                   jax.ShapeDtypeStruct((nseq, SSD_CW - 1, 2 * DBR), f32),
                   jax.ShapeDtypeStruct((nseq, SSD_H, SSD_P, SSD_N), f32)],
        scratch_shapes=[pltpu.VMEM((SSD_HDR + SSD_Q, DBR), f32), pltpu.VMEM((SSD_HDR + SSD_Q, DBR), f32),
                        pltpu.VMEM((SSD_H, SSD_P, SSD_N), f32)],
        compiler_params=_cp(("arbitrary", "arbitrary")),
        name="ssd_seq",
    )(proj, proj, proj, dt_raw, conv_w, conv_w, conv_b, conv_b, dt_bias, a_log, d_ch, norm_g)


SSD_BT = 8


def _ssd_step_kernel(z_ref, x_ref, bc_ref, dt_ref, cx_ref, cbc_ref, h0_ref, wx_ref, wbc_ref, bx_ref, bbc_ref,
                     dtb_ref, alog_ref, dch_ref, ng_ref, y_ref, hout_ref):
    def conv(raw_ref, c_ref, w_ref, b_ref):
        acc = w_ref[SSD_CW - 1:SSD_CW, :] * raw_ref[...] + b_ref[...]
        for k in range(SSD_CW - 1):
            acc = acc + w_ref[k:k + 1, :] * c_ref[k]
        return _silu(acc)

    xs = conv(x_ref, cx_ref, wx_ref, bx_ref)
    bc = conv(bc_ref, cbc_ref, wbc_ref, bbc_ref)
    dt = _softplus(dt_ref[...] + dtb_ref[...])
    da = jnp.exp(dt * (-jnp.exp(alog_ref[...])))
    rows_x = lax.broadcasted_iota(i32, (SSD_BT, SSD_GC), 0)
    rows_n = lax.broadcasted_iota(i32, (SSD_BT, SSD_N), 0)
    hpg = SSD_H // SSD_G
    y_groups = []
    for g in range(SSD_G):
        bg = bc[:, SSD_N * g:SSD_N * (g + 1)].astype(bf16)
        cg = bc[:, SSD_G * SSD_N + SSD_N * g:SSD_G * SSD_N + SSD_N * (g + 1)]
        xdt = jnp.concatenate(
            [xs[:, SSD_P * (g * hpg + r):SSD_P * (g * hpg + r + 1)] * dt[:, g * hpg + r:g * hpg + r + 1]
             for r in range(hpg)], axis=1)
        yg = jnp.zeros((SSD_BT, SSD_GC), f32)
        for b in range(SSD_BT):
            upd = _dot_tt(jnp.where(rows_x == b, xdt, 0.0).astype(bf16), bg)
            hn = []
            for r in range(hpg):
                h = g * hpg + r
                hnew = da[b:b + 1, h:h + 1] * h0_ref[b, h] + upd[SSD_P * r:SSD_P * (r + 1), :]
                hout_ref[b, h] = hnew
                hn.append(hnew)
            hn = jnp.concatenate(hn, axis=0).astype(bf16)
            yg = yg + _dot_nt(jnp.where(rows_n == b, cg, 0.0).astype(bf16), hn)
        y_groups.append(yg)
    y = jnp.concatenate(y_groups, axis=1) + dch_ref[...] * xs
    y_ref[...] = _gated_group_norm(y, z_ref[...], ng_ref[...]).astype(y_ref.dtype)


def ssd_step(proj, dt_raw, row0, cache_t, h0, conv_w, conv_b, dt_bias, a_log, d_ch, norm_g):
    rb = row0 // SSD_BT
    rowblk = lambda col: pl.BlockSpec((SSD_BT, DBR), lambda i: (rb + i, col))
    vec = lambda col: pl.BlockSpec((1, DBR), lambda i: (0, col))
    vecl = pl.BlockSpec((1, LANE), lambda i: (0, 0))
    st = pl.BlockSpec((SSD_BT, SSD_H, SSD_P, SSD_N), lambda i: (i, 0, 0, 0))
    return pl.pallas_call(
        _ssd_step_kernel,
        grid=(NS // SSD_BT,),
        in_specs=[
            rowblk(6), rowblk(7), rowblk(8),
            pl.BlockSpec((SSD_BT, LANE), lambda i: (rb + i, 0)),
            pl.BlockSpec((SSD_CW - 1, SSD_BT, DBR), lambda i: (0, i, 0)),
            pl.BlockSpec((SSD_CW - 1, SSD_BT, DBR), lambda i: (0, i, 1)),
            st,
            pl.BlockSpec((SSD_CW, DBR), lambda i: (0, 0)),
            pl.BlockSpec((SSD_CW, DBR), lambda i: (0, 1)),
            vec(0), vec(1), vecl, vecl, vec(0), vec(0),
        ],
        out_specs=[pl.BlockSpec((SSD_BT, DBR), lambda i: (i, 0)), st],
        out_shape=[jax.ShapeDtypeStruct((NS, DBR), f32),
                   jax.ShapeDtypeStruct((NS, SSD_H, SSD_P, SSD_N), f32)],
        compiler_params=_cp(("arbitrary",)),
        name="ssd_step",
    )(proj, proj, proj, dt_raw, cache_t, cache_t, h0, conv_w, conv_w, conv_b, conv_b,
      dt_bias, a_log, d_ch, norm_g)


def moe_plan(top_idx):
    flat_e = top_idx.reshape(-1)
    onehot = (flat_e[:, None] == jnp.arange(NE, dtype=i32)[None, :]).astype(i32)
    csum = jnp.cumsum(onehot, axis=0)
    rank = jnp.sum(onehot * csum, axis=1) - 1
    counts = csum[-1]
    seg = (counts + MOE_PAD - 1) // MOE_PAD * MOE_PAD
    seg0 = jnp.cumsum(seg) - seg
    units = (counts + MOE_RMAX - 1) // MOE_RMAX
    unit_end = jnp.cumsum(units)
    unit0 = unit_end - units
    xpos = seg0[flat_e] + rank
    ypos = (unit0[flat_e] + rank // MOE_RMAX) * MOE_RMAX + rank % MOE_RMAX
    tok_sorted = jnp.zeros((MOE_NP,), i32).at[xpos].set(jnp.arange(T * TOPK, dtype=i32) // TOPK)
    u = jnp.arange(MOE_NU, dtype=i32)
    n_units = unit_end[-1]
    ue = jnp.minimum(jnp.searchsorted(unit_end, u, side='right').astype(i32), NE - 1)
    last_e = ue[jnp.maximum(n_units - 1, 0)]
    valid = u < n_units
    unit_e = jnp.where(valid, ue, last_e)
    unit_row0 = jnp.where(valid, seg0[ue] + (u - unit0[ue]) * MOE_RMAX, 0)
    n_used = ((seg0[-1] + seg[-1] + MOE_GT - 1) // MOE_GT).reshape(1).astype(i32)
    return tok_sorted, n_used, ypos.reshape(T, TOPK), n_units, unit_e, unit_row0, valid.astype(i32)


def _gather_kernel(nused_ref, idx_ref, idxn_ref, h_ref, o_ref, buf_ref, sem):
    i = pl.program_id(0)
    nused = nused_ref[0]
    slot = lax.rem(i, 2)

    def issue(tok_ref, sl):
        def body(r, carry):
            pltpu.make_async_copy(h_ref.at[tok_ref[0, 0, r]], buf_ref.at[sl, r], sem.at[sl]).start()
            return carry
        lax.fori_loop(0, MOE_GT, body, 0)

    @pl.when((i == 0) & (nused > 0))
    def _():
        issue(idx_ref, 0)

    @pl.when(i + 1 < nused)
    def _():
        issue(idxn_ref, 1 - slot)

    @pl.when(i < nused)
    def _():
        pltpu.make_async_copy(h_ref.at[pl.ds(0, MOE_GT)], buf_ref.at[slot], sem.at[slot]).wait()
        o_ref[...] = buf_ref[slot].reshape(MOE_GT, D).astype(o_ref.dtype)

    @pl.when(i >= nused)
    def _():
        o_ref[...] = jnp.zeros(o_ref.shape, o_ref.dtype)


def moe_gather(h3, tok_sorted, n_used):
    nt = MOE_NP // MOE_GT
    idx = tok_sorted.reshape(nt, 1, MOE_GT)
    grid_spec = pltpu.PrefetchScalarGridSpec(
        num_scalar_prefetch=1,
        grid=(nt,),
        in_specs=[pl.BlockSpec((1, 1, MOE_GT), lambda i, nu: (i, 0, 0), memory_space=pltpu.SMEM),
                  pl.BlockSpec((1, 1, MOE_GT), lambda i, nu: (jnp.minimum(i + 1, nt - 1), 0, 0),
                               memory_space=pltpu.SMEM),
                  pl.BlockSpec(memory_space=pl.ANY)],
        out_specs=pl.BlockSpec((MOE_GT, D), lambda i, nu: (i, 0)),
        scratch_shapes=[pltpu.VMEM((2, MOE_GT, D // LANE, LANE), f32), pltpu.SemaphoreType.DMA((2,))],
    )
    return pl.pallas_call(
        _gather_kernel,
        grid_spec=grid_spec,
        out_shape=jax.ShapeDtypeStruct((MOE_NP, D), bf16),
        compiler_params=_cp(("arbitrary",)),
        name="moe_gather",
    )(n_used, idx, idx, h3)


def _ffn_kernel(ue_ref, ur_ref, uv_ref, xs_ref, wg_ref, wl_ref, bg_ref, bl_ref, w2_ref, b2_ref, o_ref,
                x_ref, act_ref, sem):
    u = pl.program_id(0)
    s = pl.program_id(1)
    valid = uv_ref[u] > 0

    @pl.when((s == 0) & valid)
    def _():
        src = xs_ref.at[pl.ds(pl.multiple_of(ur_ref[u], MOE_PAD), MOE_RMAX), :]
        cp = pltpu.make_async_copy(src, x_ref, sem)
        cp.start()
        cp.wait()

    @pl.when((s < N_P1) & valid)
    def _():
        w1 = jnp.concatenate([wg_ref[...].astype(bf16), wl_ref[...].astype(bf16)], axis=1)
        hid = _dot(x_ref[...], w1) + jnp.concatenate([bg_ref[...], bl_ref[...]], axis=1)
        glu = jnp.minimum(hid[:, 0:MOE_TF], 7.0)
        lin = jnp.clip(hid[:, MOE_TF:2 * MOE_TF], -7.0, 7.0)
        act_ref[s] = (glu * _sigmoid(1.702 * glu) * (lin + 1.0)).astype(bf16)

    @pl.when((s >= N_P1) & valid)
    def _():
        act = jnp.concatenate([act_ref[j] for j in range(N_P1)], axis=1)
        o_ref[...] = _dot(act, w2_ref[...].astype(bf16)) + b2_ref[...]

    @pl.when((s >= N_P1) & jnp.logical_not(valid))
    def _():
        o_ref[...] = jnp.zeros(o_ref.shape, f32)


def moe_ffn(xs, unit_e, unit_row0, unit_valid, w1, b1, w2, b2, l, n_units):
    def c1(u, s, uv):
        return jnp.where(uv[u] > 0, jnp.minimum(s, N_P1 - 1), N_P1 - 1)

    def c2(u, s, uv):
        return jnp.where(uv[u] > 0, jnp.maximum(s - N_P1, 0), N_P2 - 1)

    nglu = DFF // MOE_TF
    grid_spec = pltpu.PrefetchScalarGridSpec(
        num_scalar_prefetch=3,
        grid=(n_units, N_P1 + N_P2),
        in_specs=[
            pl.BlockSpec(memory_space=pl.ANY),
            pl.BlockSpec((None, None, D, MOE_TF), lambda u, s, ue, ur, uv: (l, ue[u], 0, c1(u, s, uv))),
            pl.BlockSpec((None, None, D, MOE_TF), lambda u, s, ue, ur, uv: (l, ue[u], 0, nglu + c1(u, s, uv))),
            pl.BlockSpec((None, None, 1, MOE_TF), lambda u, s, ue, ur, uv: (l, ue[u], 0, c1(u, s, uv))),
            pl.BlockSpec((None, None, 1, MOE_TF), lambda u, s, ue, ur, uv: (l, ue[u], 0, nglu + c1(u, s, uv))),
            pl.BlockSpec((None, None, DFF, MOE_TO), lambda u, s, ue, ur, uv: (l, ue[u], 0, c2(u, s, uv))),
            pl.BlockSpec((None, None, 1, MOE_TO), lambda u, s, ue, ur, uv: (l, ue[u], 0, c2(u, s, uv))),
        ],
        out_specs=pl.BlockSpec((MOE_RMAX, MOE_TO), lambda u, s, ue, ur, uv: (u, jnp.maximum(s - N_P1, 0))),
        scratch_shapes=[
            pltpu.VMEM((MOE_RMAX, D), bf16),
            pltpu.VMEM((N_P1, MOE_RMAX, MOE_TF), bf16),
            pltpu.SemaphoreType.DMA(()),
        ],
    )
    b1r = b1.reshape(DEPTH, NE, 1, 2 * DFF)
    b2r = b2.reshape(DEPTH, NE, 1, D)
    return pl.pallas_call(
        _ffn_kernel,
        grid_spec=grid_spec,
        out_shape=jax.ShapeDtypeStruct((n_units * MOE_RMAX, D), f32),
        compiler_params=_cp(("arbitrary", "arbitrary")),
        name="moe_ffn",
    )(unit_e[:n_units], unit_row0[:n_units], unit_valid[:n_units], xs, w1, w1, b1r, b1r, w2, b2r)


def moe_apply(xs, plan, w1, b1, w2, b2, l, x, mod, top_w):
    ypos, n_units, unit_e, unit_row0, unit_valid = plan

    def run(nu):
        ys = moe_ffn(xs, unit_e, unit_row0, unit_valid, w1, b1, w2, b2, l, nu)
        return moe_combine(ys, ypos, x, mod, top_w)

    return lax.cond(n_units <= MOE_NU_SMALL, lambda: run(MOE_NU_SMALL), lambda: run(MOE_NU))


def _combine_kernel(pos_ref, ys_ref, x_ref, g2_ref, tw_ref, o_ref, buf_ref, sem):
    def issue(i, carry):
        for k in range(TOPK):
            p = pos_ref[0, 0, i * TOPK + k]
            pltpu.make_async_copy(ys_ref.at[pl.ds(p, 1), :], buf_ref.at[k, pl.ds(i, 1), :], sem).start()
        return carry
    lax.fori_loop(0, ROW_T, issue, 0)
    for k in range(TOPK):
        pltpu.make_async_copy(ys_ref.at[pl.ds(0, ROW_T), :], buf_ref.at[k], sem).wait()
    tw = tw_ref[...]
    y = tw[:, 0:1] * buf_ref[0]
    for k in range(1, TOPK):
        y = y + tw[:, k:k + 1] * buf_ref[k]
    o_ref[...] = x_ref[...] + g2_ref[...] * y


def moe_combine(ys, ypos, x, mod, top_w):
    nt = T // ROW_T
    return pl.pallas_call(
        _combine_kernel,
        grid=(nt,),
        in_specs=[
            pl.BlockSpec((1, 1, ROW_T * TOPK), lambda i: (i, 0, 0), memory_space=pltpu.SMEM),
            pl.BlockSpec(memory_space=pl.ANY),
            pl.BlockSpec((ROW_T, D), lambda i: (i, 0)),
            pl.BlockSpec((ROW_T, D), lambda i: (_mod_group(i), 5)),
            pl.BlockSpec((ROW_T, LANE), lambda i: (i, 0)),
        ],
        out_specs=pl.BlockSpec((ROW_T, D), lambda i: (i, 0)),
        out_shape=jax.ShapeDtypeStruct((T, D), f32),
        scratch_shapes=[pltpu.VMEM((TOPK, ROW_T, D), f32), pltpu.SemaphoreType.DMA(())],
        compiler_params=_cp(("arbitrary",)),
        name="moe_combine",
    )(ypos.reshape(nt, 1, ROW_T * TOPK), ys, x, mod, top_w)


def _pad_lanes(w):
    return jnp.pad(w, ((0, 0), (0, LANE - w.shape[1])))


def kernel(x_prompt, x_sample, state_s5_re, state_s5_im, cache_sconv, cache_conformer, cache_ssd_conv, state_ssd,
           c_prompt, c_sample, w_ada, b_ada, norm_mix, norm_ffn, w_in,
           s5_lambda_re, s5_lambda_im, s5_log_step, s5_b_re, s5_b_im, s5_c_re, s5_c_im, s5_d, s5_w_glu, w_out_a,
           sconv_w, w_out_b, conf_w, conf_b, conf_ln_g, conf_ln_b, w_out_c,
           ssd_conv_w, ssd_conv_b, ssd_dt_bias, ssd_a_log, ssd_d, ssd_norm, w_out_d, w_o,
           router_w, router_b, moe_w1, moe_b1, moe_w2, moe_b2, norm_final):
    x = jnp.concatenate([x_prompt.reshape(TP, D), x_sample.reshape(NS, D)], axis=0)
    c_all = jnp.concatenate([c_prompt, jnp.zeros((16 - NB, D), f32), c_sample], axis=0)
    srow = TP // NS
    outs_p = [[] for _ in range(6)]
    outs_s = [[] for _ in range(6)]
    for l in range(DEPTH):
        mod_c = ada_mod(c_all, w_ada, b_ada, l)
        mod = jnp.concatenate([jnp.repeat(mod_c[:NB], ROW_T, axis=0), mod_c[16:]], axis=0)
        w_dt = _pad_lanes(w_in[l, :, NMIX:NMIX + NDT]).astype(bf16)
        h, dt_raw = normmod_mix(x, norm_mix[l][None], mod, w_dt)
        proj = in_proj(h, w_in, l)
        gates = gate_proj(h, w_in, l)

        a_re, a_im, bb_re, bb_im = s5_params(s5_lambda_re[l], s5_lambda_im[l], s5_log_step[l], s5_b_re[l], s5_b_im[l])
        bsup, csup = _s5_slab_weights(bb_re, bb_im, s5_c_re[l], s5_c_im[l])
        d_a = s5_d[l][None]
        ya_p, s5r_p, s5i_p = s5_seq(proj, bsup, csup, a_re, a_im, d_a)
        ya_s, s5r_s, s5i_s = s5_step(proj, srow, state_s5_re[l].reshape(NS, S5G * S5P),
                                     state_s5_im[l].reshape(NS, S5G * S5P), bsup, csup, a_re, a_im, d_a)
        y_a = s5_glu(jnp.concatenate([ya_p, ya_s], axis=0), s5_w_glu, l)

        yb_p, sconv_p = sconv_seq(proj, sconv_w[l])
        yb_s, p_s = sconv_step(proj, srow, cache_sconv[l][:, 0], cache_sconv[l][:, 1], sconv_w[l])
        y_b = jnp.concatenate([yb_p, yb_s.astype(bf16)], axis=0)
        sconv_s = jnp.stack([cache_sconv[l][:, 1], p_s], axis=1)

        cvec = (conf_b[l][None], conf_ln_g[l][None], conf_ln_b[l][None])
        yc_p, conf_p = conf_seq(proj, conf_w[l], *cvec)
        yc_s, glu_s = conf_step(proj, TP, cache_conformer[l], conf_w[l], *cvec)
        y_c = jnp.concatenate([yc_p, yc_s.astype(bf16)], axis=0)
        conf_s = jnp.concatenate([cache_conformer[l][:, 1:], glu_s[:, None]], axis=1)

        ssd_vec = (ssd_conv_w[l], ssd_conv_b[l][None], _pad_lanes(ssd_dt_bias[l][None]), _pad_lanes(ssd_a_log[l][None]),
                   jnp.repeat(ssd_d[l], SSD_P)[None], ssd_norm[l][None])
        yd_p, ssdc_p, ssdh_p = ssd_seq(proj, dt_raw, *ssd_vec)
        yd_s, ssdh_s = ssd_step(proj, dt_raw, TP, jnp.transpose(cache_ssd_conv[l], (1, 0, 2)), state_ssd[l], *ssd_vec)
        y_d = jnp.concatenate([yd_p, yd_s.astype(bf16)], axis=0)
        ssdc_s = jnp.concatenate([cache_ssd_conv[l][:, 1:], proj[TP:, 7 * DBR:9 * DBR][:, None]], axis=1)

        merged = merge_branches((y_a, y_b, y_c, y_d), (w_out_a, w_out_b, w_out_c, w_out_d), gates, l)
        o = out_proj(merged, w_o, l)

        w_r = _pad_lanes(router_w[l])
        w_r_hi = w_r.astype(bf16)
        w_r_lo = (w_r - w_r_hi.astype(f32)).astype(bf16)
        b_r = _pad_lanes(router_b[l][None])
        x1, h2, top_i, top_w = normmod_ffn(x, o, norm_ffn[l][None], mod, w_r_hi, w_r_lo, b_r)
        tok_sorted, n_used, *plan = moe_plan(top_i[:, :TOPK])
        xs = moe_gather(h2, tok_sorted, n_used)
        x = moe_apply(xs, plan, moe_w1, moe_b1, moe_w2, moe_b2, l, x1, mod, top_w)

        for lst, v in zip(outs_p, (s5r_p, s5i_p, sconv_p, conf_p, ssdc_p, ssdh_p)):
            lst.append(v)
        for lst, v in zip(outs_s, (s5r_s.reshape(NS, S5G, S5P), s5i_s.reshape(NS, S5G, S5P), sconv_s, conf_s, ssdc_s, ssdh_s)):
            lst.append(v)

    y = final_norm(x, norm_final[None])
    p = [jnp.stack(a, axis=0) for a in outs_p]
    s = [jnp.stack(a, axis=0) for a in outs_s]
    return (y[:TP].reshape(NB, L, D), y[TP:].reshape(NS, 1, D),
            p[0], p[1], p[2], p[3], p[4], p[5], s[0], s[1], s[2], s[3], s[4], s[5])
```
